```python
import jax, jax.numpy as jnp
from jax import lax
import numpy as np

D_MODEL = 2048
BATCH = 4
SEQ = 4096
DEPTH = 4

N_META = 16
CONV_W = 4
D_RNN = D_MODEL // 2
RG_BLOCKS = 8
RG_BS = D_RNN // RG_BLOCKS
RG_C = 8.0
GDN_DK = 128
GDN_DV = 128
GDN_H_QK = 4
GDN_H_V = 8
GDN_QK_DIM = GDN_H_QK * GDN_DK
GDN_V_DIM = GDN_H_V * GDN_DV
CHUNK = 64
N_EXPERTS = 32
TOP_K = 4
D_EXPERT = D_MODEL // 2
SWIGLU_LIMIT = 7.0
SWIGLU_ALPHA = 1.702
MOE_BLOCK = 128
LN_EPS = 1e-5
RMS_EPS = 1e-6
DEEPNORM_ALPHA = (2 * DEPTH) ** 0.25
DEEPNORM_BETA = (8 * DEPTH) ** -0.25
IN_SPLITS = (D_RNN, D_RNN, GDN_QK_DIM, GDN_QK_DIM, GDN_V_DIM, GDN_V_DIM, GDN_H_V, GDN_H_V, D_MODEL, D_MODEL)
D_IN = sum(IN_SPLITS)

kernel_name = "hybrid_rglru_gdn_moe_deepnorm"

F32 = jnp.float32


def layer_norm(x, g, b):
    xf = x.astype(F32)
    mu = jnp.mean(xf, axis=-1, keepdims=True)
    xc = xf - mu
    var = jnp.mean(xc * xc, axis=-1, keepdims=True)
    return (xc * lax.rsqrt(var + LN_EPS) * g.astype(F32) + b.astype(F32)).astype(x.dtype)


def causal_dwconv(x, w):
    T = x.shape[1]
    xp = jnp.pad(x, ((0, 0), (CONV_W - 1, 0), (0, 0)))
    y = xp[:, 0:T] * w[0]
    for j in range(1, CONV_W):
        y = y + xp[:, j:j + T] * w[j]
    return y


def _linear_recurrence_combine(left, right):
    a1, b1 = left
    a2, b2 = right
    return a1 * a2, a2 * b1 + b2


def rglru_branch(xr, yr, conv_w, conv_b, wx, bx, wa, ba, lam):
    B, T, _ = xr.shape
    xc = causal_dwconv(xr, conv_w) + conv_b
    xh = xc.reshape(B, T, RG_BLOCKS, RG_BS)
    gi = jax.nn.sigmoid(jnp.einsum('bthi,hij->bthj', xh, wx).reshape(B, T, D_RNN) + bx).astype(F32)
    gr = jax.nn.sigmoid(jnp.einsum('bthi,hij->bthj', xh, wa).reshape(B, T, D_RNN) + ba).astype(F32)
    log_a = -RG_C * gr * jax.nn.softplus(-lam.astype(F32))
    a = jnp.exp(log_a)
    u = jnp.sqrt(-jnp.expm1(2.0 * log_a)) * gi * xc.astype(F32)
    _, h = lax.associative_scan(_linear_recurrence_combine, (a, u), axis=1)
    return (h * jax.nn.gelu(yr.astype(F32))).astype(xr.dtype)


def l2norm(x):
    return x * lax.rsqrt(jnp.sum(x * x, axis=-1, keepdims=True) + RMS_EPS)


def chunked_gated_delta_rule(q, k, v, g, beta):
    B, T, H, _ = q.shape
    pad = (-T) % CHUNK

    def to_chunks(t):
        t = jnp.pad(t, ((0, 0), (pad, 0)) + ((0, 0),) * (t.ndim - 2))
        nc = t.shape[1] // CHUNK
        t = t.reshape((B, nc, CHUNK) + t.shape[2:])
        return jnp.moveaxis(t, 3, 1)

    q, k, v, g, beta = (to_chunks(t) for t in (q, k, v, g, beta))
    nc = q.shape[2]
    G = jnp.cumsum(g, axis=-1)
    causal = jnp.tril(jnp.ones((CHUNK, CHUNK), dtype=bool))
    strict = jnp.tril(jnp.ones((CHUNK, CHUNK), dtype=bool), -1)
    decay = jnp.exp(jnp.where(causal, G[..., :, None] - G[..., None, :], -jnp.inf))
    kb = k * beta[..., None]
    L = jnp.where(strict, jnp.einsum('bhncd,bhnsd->bhncs', kb, k) * decay, 0.0)
    rhs = jnp.concatenate([v * beta[..., None], kb * jnp.exp(G)[..., None]], axis=-1)
    sol = lax.linalg.triangular_solve(L, rhs, left_side=True, lower=True, unit_diagonal=True)
    value, k_cum = sol[..., :GDN_DV], sol[..., GDN_DV:]
    attn = jnp.einsum('bhncd,bhnsd->bhncs', q, k) * decay
    q_g = q * jnp.exp(G)[..., None]
    G_last = G[..., -1]
    k_s = k * jnp.exp(G_last[..., None] - G)[..., None]

    def step(S, inp):
        q_c, ks_c, val_c, kc_c, attn_c, gl_c = inp
        v_new = val_c - jnp.einsum('bhcd,bhde->bhce', kc_c, S)
        o = jnp.einsum('bhcd,bhde->bhce', q_c, S) + jnp.einsum('bhcs,bhse->bhce', attn_c, v_new)
        S = S * jnp.exp(gl_c)[..., None, None] + jnp.einsum('bhcd,bhce->bhde', ks_c, v_new)
        return S, o

    xs = (jnp.moveaxis(q_g, 2, 0), jnp.moveaxis(k_s, 2, 0), jnp.moveaxis(value, 2, 0),
          jnp.moveaxis(k_cum, 2, 0), jnp.moveaxis(attn, 2, 0), jnp.moveaxis(G_last, 2, 0))
    S0 = jnp.zeros((B, H, GDN_DK, GDN_DV), F32)
    _, o = lax.scan(step, S0, xs)
    o = jnp.transpose(o, (1, 0, 3, 2, 4)).reshape(B, nc * CHUNK, H, GDN_DV)
    return o[:, pad:]


def gdn_branch(q, k, v, z, b_beta, a_dec, conv_w, a_log, dt_bias, norm_w):
    B, T, _ = q.shape
    qkv = jax.nn.silu(causal_dwconv(jnp.concatenate([q, k, v], axis=-1), conv_w)).astype(F32)
    qc = l2norm(qkv[..., :GDN_QK_DIM].reshape(B, T, GDN_H_QK, GDN_DK))
    kc = l2norm(qkv[..., GDN_QK_DIM:2 * GDN_QK_DIM].reshape(B, T, GDN_H_QK, GDN_DK))
    vc = qkv[..., 2 * GDN_QK_DIM:].reshape(B, T, GDN_H_V, GDN_DV)
    rep = GDN_H_V // GDN_H_QK
    qc = jnp.repeat(qc, rep, axis=2) * (GDN_DK ** -0.5)
    kc = jnp.repeat(kc, rep, axis=2)
    beta = jax.nn.sigmoid(b_beta.astype(F32))
    g = -jnp.exp(a_log.astype(F32)) * jax.nn.softplus(a_dec.astype(F32) + dt_bias.astype(F32))
    o = chunked_gated_delta_rule(qc, kc, vc, g, beta)
    o = o * lax.rsqrt(jnp.mean(o * o, axis=-1, keepdims=True) + RMS_EPS) * norm_w.astype(F32)
    o = o * jax.nn.silu(z.astype(F32).reshape(B, T, GDN_H_V, GDN_DV))
    return o.reshape(B, T, GDN_V_DIM).astype(q.dtype)


def routed_moe(x2d, w_router, b_router, w_gu, b_gu, w_dn, b_dn):
    n, d = x2d.shape
    logits = (x2d @ w_router + b_router).astype(F32)
    top_val, top_idx = lax.top_k(logits, TOP_K)
    gate = jax.nn.softmax(top_val, axis=-1).astype(x2d.dtype)
    nk = n * TOP_K
    e_flat = top_idx.reshape(-1)
    tok_flat = jnp.arange(nk, dtype=jnp.int32) // TOP_K
    w_flat = gate.reshape(-1)
    order = jnp.argsort(e_flat)
    e_sorted = e_flat[order]
    counts = jnp.zeros((N_EXPERTS,), jnp.int32).at[e_flat].add(1)
    starts = jnp.cumsum(counts) - counts
    padded = (counts + MOE_BLOCK - 1) // MOE_BLOCK * MOE_BLOCK
    pad_ends = jnp.cumsum(padded)
    pad_starts = pad_ends - padded
    dest = pad_starts[e_sorted] + (jnp.arange(nk, dtype=jnp.int32) - starts[e_sorted])
    n_blocks = -(-nk // MOE_BLOCK) + N_EXPERTS
    m_pad = n_blocks * MOE_BLOCK
    slot_tok = jnp.full((m_pad,), n, jnp.int32).at[dest].set(tok_flat[order])
    slot_w = jnp.zeros((m_pad,), x2d.dtype).at[dest].set(w_flat[order])
    block_start = jnp.arange(n_blocks, dtype=jnp.int32) * MOE_BLOCK
    block_exp = jnp.minimum(jnp.searchsorted(pad_ends, block_start, side='right'), N_EXPERTS - 1)
    x_pad = jnp.concatenate([x2d, jnp.zeros((1, d), x2d.dtype)], axis=0)
    xb = x_pad[slot_tok].reshape(n_blocks, MOE_BLOCK, d)

    def block_ffn(args):
        xs, e = args
        hgu = xs @ w_gu[e] + b_gu[e]
        x_glu = jnp.minimum(hgu[..., ::2], SWIGLU_LIMIT)
        x_lin = jnp.clip(hgu[..., 1::2], -SWIGLU_LIMIT, SWIGLU_LIMIT)
        act = x_glu * jax.nn.sigmoid(SWIGLU_ALPHA * x_glu) * (x_lin + 1.0)
        return act @ w_dn[e] + b_dn[e]

    yb = lax.map(block_ffn, (xb, block_exp))
    y = yb.reshape(m_pad, d) * slot_w[:, None]
    return jax.ops.segment_sum(y, slot_tok, num_segments=n + 1)[:n]


def hybrid_layer(h, w_in, rg_conv_w, rg_conv_b, rg_wx, rg_bx, rg_wa, rg_ba, rg_lambda,
                 gdn_conv_w, gdn_a_log, gdn_dt_bias, gdn_norm_w, w_branch_a, w_branch_b, w_out,
                 ln1_g, ln1_b, w_router, b_router, w_gate_up, b_gate_up, w_down, b_down, ln2_g, ln2_b):
    B, T, D = h.shape
    proj = h @ w_in
    idx = np.cumsum(IN_SPLITS)[:-1].tolist()
    rg_x, rg_y, q, k, v, z, b_beta, a_dec, gate_a, gate_b = jnp.split(proj, idx, axis=-1)
    y_a = rglru_branch(rg_x, rg_y, rg_conv_w, rg_conv_b, rg_wx, rg_bx, rg_wa, rg_ba, rg_lambda)
    y_b = gdn_branch(q, k, v, z, b_beta, a_dec, gdn_conv_w, gdn_a_log, gdn_dt_bias, gdn_norm_w)
    mixed = jax.nn.sigmoid(gate_a) * (y_a @ w_branch_a) + jax.nn.sigmoid(gate_b) * (y_b @ w_branch_b)
    h = layer_norm(DEEPNORM_ALPHA * h + mixed @ w_out, ln1_g, ln1_b)
    moe_out = routed_moe(h.reshape(B * T, D), w_router, b_router, w_gate_up, b_gate_up,
                         w_down, b_down).reshape(B, T, D)
    return layer_norm(DEEPNORM_ALPHA * h + moe_out, ln2_g, ln2_b)


def setup_inputs(seed: int = 0) -> dict:
    key = jax.random.key(seed)
    ks = jax.random.split(key, 32)

    def nrm(k, shape, scale):
        return jax.random.normal(k, shape, F32) * scale

    D = D_MODEL
    u = jax.random.uniform(ks[11], (DEPTH, D_RNN), F32, minval=0.9, maxval=0.999)
    a0 = u ** (1.0 / RG_C)
    dt = jnp.exp(jax.random.uniform(ks[14], (DEPTH, GDN_H_V), F32, minval=np.log(0.001), maxval=np.log(0.1)))
    return {
        'x': nrm(ks[0], (BATCH, SEQ, D), 1.0),
        'meta_tokens': nrm(ks[1], (N_META, D), 1.0),
        'ln_in_g': 1.0 + nrm(ks[2], (D,), 0.02),
        'ln_in_b': nrm(ks[3], (D,), 0.02),
        'w_in': nrm(ks[4], (DEPTH, D, D_IN), D ** -0.5),
        'rg_conv_w': nrm(ks[5], (DEPTH, CONV_W, D_RNN), CONV_W ** -0.5),
        'rg_conv_b': nrm(ks[6], (DEPTH, D_RNN), 0.02),
        'rg_wx': nrm(ks[7], (DEPTH, RG_BLOCKS, RG_BS, RG_BS), RG_BS ** -0.5),
        'rg_bx': nrm(ks[8], (DEPTH, D_RNN), 0.02),
        'rg_wa': nrm(ks[9], (DEPTH, RG_BLOCKS, RG_BS, RG_BS), RG_BS ** -0.5),
        'rg_ba': nrm(ks[10], (DEPTH, D_RNN), 0.02),
        'rg_lambda': jnp.log(a0) - jnp.log1p(-a0),
        'gdn_conv_w': nrm(ks[12], (DEPTH, CONV_W, 2 * GDN_QK_DIM + GDN_V_DIM), CONV_W ** -0.5),
        'gdn_a_log': jnp.log(jax.random.uniform(ks[13], (DEPTH, GDN_H_V), F32, minval=1.0, maxval=16.0)),
        'gdn_dt_bias': dt + jnp.log(-jnp.expm1(-dt)),
        'gdn_norm_w': 1.0 + nrm(ks[15], (DEPTH, GDN_DV), 0.02),
        'w_branch_a': nrm(ks[16], (DEPTH, D_RNN, D), D_RNN ** -0.5),
        'w_branch_b': nrm(ks[17], (DEPTH, GDN_V_DIM, D), GDN_V_DIM ** -0.5),
        'w_out': nrm(ks[18], (DEPTH, D, D), D ** -0.5 * DEEPNORM_BETA),
        'ln1_g': 1.0 + nrm(ks[19], (DEPTH, D), 0.02),
        'ln1_b': nrm(ks[20], (DEPTH, D), 0.02),
        'w_router': nrm(ks[21], (DEPTH, D, N_EXPERTS), D ** -0.5),
        'b_router': nrm(ks[22], (DEPTH, N_EXPERTS), 0.01),
        'w_gate_up': nrm(ks[23], (DEPTH, N_EXPERTS, D, 2 * D_EXPERT), D ** -0.5),
        'b_gate_up': nrm(ks[24], (DEPTH, N_EXPERTS, 2 * D_EXPERT), 0.01),
        'w_down': nrm(ks[25], (DEPTH, N_EXPERTS, D_EXPERT, D), D_EXPERT ** -0.5 * DEEPNORM_BETA),
        'b_down': nrm(ks[26], (DEPTH, N_EXPERTS, D), 0.01),
        'ln2_g': 1.0 + nrm(ks[27], (DEPTH, D), 0.02),
        'ln2_b': nrm(ks[28], (DEPTH, D), 0.02),
    }


def reference(x, meta_tokens, ln_in_g, ln_in_b, w_in, rg_conv_w, rg_conv_b, rg_wx, rg_bx, rg_wa, rg_ba,
              rg_lambda, gdn_conv_w, gdn_a_log, gdn_dt_bias, gdn_norm_w, w_branch_a, w_branch_b, w_out,
              ln1_g, ln1_b, w_router, b_router, w_gate_up, b_gate_up, w_down, b_down, ln2_g, ln2_b):
    B = x.shape[0]
    meta = jnp.broadcast_to(meta_tokens[None].astype(x.dtype), (B, N_META, D_MODEL))
    h = jnp.concatenate([meta, x], axis=1)
    h = layer_norm(h, ln_in_g, ln_in_b)
    for l in range(DEPTH):
        h = hybrid_layer(h, w_in[l], rg_conv_w[l], rg_conv_b[l], rg_wx[l], rg_bx[l], rg_wa[l], rg_ba[l],
                         rg_lambda[l], gdn_conv_w[l], gdn_a_log[l], gdn_dt_bias[l], gdn_norm_w[l],
                         w_branch_a[l], w_branch_b[l], w_out[l], ln1_g[l], ln1_b[l], w_router[l],
                         b_router[l], w_gate_up[l], b_gate_up[l], w_down[l], b_down[l], ln2_g[l], ln2_b[l])
    return h[:, N_META:]
```

```python
import functools

import jax
import jax.numpy as jnp
from jax import lax
from jax.experimental import pallas as pl
from jax.experimental.pallas import tpu as pltpu

F32 = jnp.float32
BF16 = jnp.bfloat16
I32 = jnp.int32

N_META = 16
CONV_W = 4
CHUNK = 64
TOP_K = 4
RG_BLOCKS = 8
RG_C = 8.0
GDN_DK = 128
GDN_DV = 128
GDN_H_QK = 4
GDN_H_V = 8
GDN_QK_DIM = GDN_H_QK * GDN_DK
GDN_V_DIM = GDN_H_V * GDN_DV
SWIGLU_LIMIT = 7.0
SWIGLU_ALPHA = 1.702
LN_EPS = 1e-5
RMS_EPS = 1e-6

LANES = 128
SUBLANES = 8
MIB = 1024 * 1024
HIGHEST = lax.Precision.HIGHEST


def _pick_tile(n, target, mult=16):
    best = None
    for t in range(mult, min(n, target) + 1, mult):
        if n % t == 0:
            best = t
    assert best is not None, (n, target, mult)
    return best


def _params(sem, vmem_mib):
    return pltpu.CompilerParams(dimension_semantics=sem, vmem_limit_bytes=vmem_mib * MIB)


def _layer_norm(x, g, b):
    mu = jnp.mean(x, axis=-1, keepdims=True)
    xc = x - mu
    var = jnp.mean(xc * xc, axis=-1, keepdims=True)
    return xc * lax.rsqrt(var + LN_EPS) * g + b


def _sigmoid(x):
    return 1.0 / (1.0 + jnp.exp(-x))


def _softplus(x):
    return jnp.maximum(x, 0.0) + jnp.log1p(jnp.exp(-jnp.abs(x)))


def _silu(x):
    return x * _sigmoid(x)


def _gelu_tanh(x):
    return 0.5 * x * (1.0 + jnp.tanh(0.7978845608028654 * (x + 0.044715 * x * x * x)))


def _ln_in_kernel(x_ref, g_ref, b_ref, of_ref, ob_ref):
    y = _layer_norm(x_ref[...], g_ref[...], b_ref[...])
    of_ref[...] = y
    ob_ref[...] = y.astype(BF16)


def _ln_in(x2d, g, b, tm):
    R, D = x2d.shape
    row = pl.BlockSpec((tm, D), lambda i: (i, 0))
    vec = pl.BlockSpec((1, D), lambda i: (0, 0))
    return pl.pallas_call(
        _ln_in_kernel,
        out_shape=(jax.ShapeDtypeStruct((R, D), F32), jax.ShapeDtypeStruct((R, D), BF16)),
        grid=(R // tm,),
        in_specs=[row, vec, vec],
        out_specs=(row, row),
        compiler_params=_params(("arbitrary",), 32),
        name="ln_in",
    )(x2d, g.reshape(1, D), b.reshape(1, D))


def _inproj_kernel(x_ref, w_ref, ws_ref, o_ref, os_ref):
    x = x_ref[...]
    o_ref[...] = jnp.dot(x, w_ref[...], preferred_element_type=F32)

    @pl.when(pl.program_id(1) == 0)
    def _():
        os_ref[...] = jnp.dot(x, ws_ref[...], preferred_element_type=F32)


def _inproj(hb, w_main, w_small, tm, tn):
    R, D = hb.shape
    N = w_main.shape[1]
    return pl.pallas_call(
        _inproj_kernel,
        out_shape=(jax.ShapeDtypeStruct((R, N), F32), jax.ShapeDtypeStruct((R, LANES), F32)),
        grid=(R // tm, N // tn),
        in_specs=[
            pl.BlockSpec((tm, D), lambda i, j: (i, 0)),
            pl.BlockSpec((D, tn), lambda i, j: (0, j)),
            pl.BlockSpec((D, LANES), lambda i, j: (0, 0)),
        ],
        out_specs=(
            pl.BlockSpec((tm, tn), lambda i, j: (i, j)),
            pl.BlockSpec((tm, LANES), lambda i, j: (i, 0)),
        ),
        compiler_params=_params(("arbitrary", "arbitrary"), 48),
        name="inproj",
    )(hb, w_main, w_small)


def _causal_conv(x, xbuf, cw_ref, tt):
    xbuf[SUBLANES:SUBLANES + tt, :] = x
    y = cw_ref[CONV_W - 1:CONV_W, :] * x
    for j in range(CONV_W - 1):
        off = SUBLANES - (CONV_W - 1) + j
        y = y + cw_ref[j:j + 1, :] * xbuf[off:off + tt, :]
    xbuf[0:SUBLANES, :] = xbuf[tt:tt + SUBLANES, :]
    return y


def _rglru_kernel(x_ref, y_ref, cw_ref, cb_ref, wx_ref, bx_ref, wa_ref, ba_ref, lam_ref,
                  o_ref, xbuf, a_s, u_s, hc, *, pad):
    t = pl.program_id(1)
    tt, C = x_ref.shape
    bs = C // RG_BLOCKS

    @pl.when(t == 0)
    def _():
        xbuf[0:SUBLANES, :] = jnp.zeros((SUBLANES, C), F32)
        hc[...] = jnp.zeros((SUBLANES, C), F32)

    row = lax.broadcasted_iota(I32, (tt, 1), 0)
    valid = (row + t * tt) >= pad
    x = jnp.where(valid, x_ref[...], 0.0)
    xc = _causal_conv(x, xbuf, cw_ref, tt) + cb_ref[...]

    xcb = xc.astype(BF16)
    gi = jnp.concatenate(
        [jnp.dot(xcb[:, h * bs:(h + 1) * bs], wx_ref[h], preferred_element_type=F32) for h in range(RG_BLOCKS)],
        axis=1)
    gr = jnp.concatenate(
        [jnp.dot(xcb[:, h * bs:(h + 1) * bs], wa_ref[h], preferred_element_type=F32) for h in range(RG_BLOCKS)],
        axis=1)
    gi = _sigmoid(gi + bx_ref[...])
    gr = _sigmoid(gr + ba_ref[...])
    log_a = (-RG_C) * gr * _softplus(-lam_ref[...])
    a = jnp.exp(log_a)
    u = jnp.sqrt(-jnp.tanh(log_a) * (a * a + 1.0)) * gi * xc
    u = jnp.where(valid, u, 0.0)

    rowmod = row % SUBLANES
    for d in (1, 2, 4):
        m = rowmod >= d
        a_sh = pltpu.roll(a, d, 0)
        u_sh = pltpu.roll(u, d, 0)
        u = jnp.where(m, a * u_sh + u, u)
        a = jnp.where(m, a * a_sh, a)
    a_s[...] = a
    u_s[...] = u

    def group(g, hprev):
        r0 = pl.multiple_of(g * SUBLANES, SUBLANES)
        h = a_s[pl.ds(r0, SUBLANES), :] * hprev + u_s[pl.ds(r0, SUBLANES), :]
        u_s[pl.ds(r0, SUBLANES), :] = h
        return jnp.broadcast_to(h[SUBLANES - 1:SUBLANES, :], (SUBLANES, C))

    hc[...] = lax.fori_loop(0, tt // SUBLANES, group, hc[...])
    o_ref[...] = (u_s[...] * _gelu_tanh(y_ref[...])).astype(BF16)


def _rglru(proj, cw, cb, wx, bx, wa, ba, lam, B, Tp, pad, tt, xcol, ycol):
    R = proj.shape[0]
    C = cw.shape[1]
    nT = Tp // tt
    vec = pl.BlockSpec((1, C), lambda b, t: (0, 0))
    wspec = pl.BlockSpec(wx.shape, lambda b, t: (0, 0, 0))
    return pl.pallas_call(
        functools.partial(_rglru_kernel, pad=pad),
        out_shape=jax.ShapeDtypeStruct((R, C), BF16),
        grid=(B, nT),
        in_specs=[
            pl.BlockSpec((tt, C), lambda b, t: (b * nT + t, xcol)),
            pl.BlockSpec((tt, C), lambda b, t: (b * nT + t, ycol)),
            pl.BlockSpec((CONV_W, C), lambda b, t: (0, 0)),
            vec, wspec, vec, wspec, vec, vec,
        ],
        out_specs=pl.BlockSpec((tt, C), lambda b, t: (b * nT + t, 0)),
        scratch_shapes=[
            pltpu.VMEM((tt + SUBLANES, C), F32),
            pltpu.VMEM((tt, C), F32),
            pltpu.VMEM((tt, C), F32),
            pltpu.VMEM((SUBLANES, C), F32),
        ],
        compiler_params=_params(("arbitrary", "arbitrary"), 48),
        name="rglru",
    )(proj, proj, cw, cb.reshape(1, C), wx, bx.reshape(1, C), wa, ba.reshape(1, C), lam.reshape(1, C))


def _unit_lower_inverse(L, masks):
    eye = masks[0]
    dinv = eye - L * masks[1]
    for m in masks[2:]:
        c = L * m
        dinv = dinv - jnp.dot(jnp.dot(dinv, c, precision=HIGHEST, preferred_element_type=F32), dinv,
                              precision=HIGHEST, preferred_element_type=F32)
    return dinv


def _gdn_kernel(qkv_ref, z_ref, sm_ref, cw_ref, alog_ref, dtb_ref, nw_ref,
                o_ref, xbuf, q_s, k_s, v_s, b_s, g_s, S_ref, *, pad):
    t = pl.program_id(1)
    tt, CQ = qkv_ref.shape
    ncht = tt // CHUNK
    rep = GDN_H_V // GDN_H_QK

    @pl.when(t == 0)
    def _():
        xbuf[0:SUBLANES, :] = jnp.zeros((SUBLANES, CQ), F32)
        S_ref[...] = jnp.zeros(S_ref.shape, F32)

    row = lax.broadcasted_iota(I32, (tt, 1), 0)
    valid = (row + t * tt) >= pad
    x = jnp.where(valid, qkv_ref[...], 0.0)
    xc = _silu(_causal_conv(x, xbuf, cw_ref, tt))

    for hq in range(GDN_H_QK):
        qh = xc[:, hq * GDN_DK:(hq + 1) * GDN_DK]
        kh = xc[:, GDN_QK_DIM + hq * GDN_DK:GDN_QK_DIM + (hq + 1) * GDN_DK]
        qn = qh * lax.rsqrt(jnp.sum(qh * qh, axis=-1, keepdims=True) + RMS_EPS) * (GDN_DK ** -0.5)
        kn = kh * lax.rsqrt(jnp.sum(kh * kh, axis=-1, keepdims=True) + RMS_EPS)
        q_s[:, hq * GDN_DK:(hq + 1) * GDN_DK] = qn
        k_s[:, hq * GDN_DK:(hq + 1) * GDN_DK] = kn
    v_s[...] = xc[:, 2 * GDN_QK_DIM:]

    sm = sm_ref[...]
    b_s[...] = _sigmoid(sm)
    g = -jnp.exp(alog_ref[...]) * _softplus(sm + dtb_ref[...])
    g_s[...] = jnp.where(valid, g, 0.0)

    ri = lax.broadcasted_iota(I32, (CHUNK, CHUNK), 0)
    ci = lax.broadcasted_iota(I32, (CHUNK, CHUNK), 1)
    causal = ri >= ci
    strict = ri > ci
    tril = causal.astype(F32)
    masks = [(ri == ci).astype(F32)]
    s = 1
    while s < CHUNK:
        masks.append(((ri // (2 * s) == ci // (2 * s)) & (ri % (2 * s) >= s) & (ci % (2 * s) < s)).astype(F32))
        s *= 2
    nw = nw_ref[...]

    def chunk(c, carry):
        r0 = pl.multiple_of(c * CHUNK, CHUNK)
        rows = pl.ds(r0, CHUNK)
        G = jnp.dot(tril, g_s[rows, :], precision=HIGHEST, preferred_element_type=F32)
        GT = G.T
        bet = b_s[rows, :]
        zc = z_ref[rows, :]
        for h in range(GDN_H_V):
            hq = h // rep
            qh = q_s[rows, hq * GDN_DK:(hq + 1) * GDN_DK]
            kh = k_s[rows, hq * GDN_DK:(hq + 1) * GDN_DK]
            vh = v_s[rows, h * GDN_DV:(h + 1) * GDN_DV]
            bcol = bet[:, h:h + 1]
            Gc = G[:, GDN_H_V + h:GDN_H_V + h + 1]
            Gr = GT[GDN_H_V + h:GDN_H_V + h + 1, :]
            decay = jnp.where(causal, jnp.exp(jnp.minimum(Gc - Gr, 0.0)), 0.0)
            eG = jnp.exp(Gc)
            g_last = Gc[CHUNK - 1:CHUNK, :]
            kb = kh * bcol
            khb = kh.astype(BF16)
            kk = lax.dot_general(kb.astype(BF16), khb, (((1,), (1,)), ((), ())), preferred_element_type=F32)
            L = jnp.where(strict, kk * decay, 0.0)
            T = _unit_lower_inverse(L, masks)
            rhs = jnp.concatenate([vh * bcol, kb * eG], axis=1).astype(BF16)
            sol = jnp.dot(T.astype(BF16), rhs, preferred_element_type=F32)
            value = sol[:, :GDN_DV]
            k_cum = sol[:, GDN_DV:]
            attn = lax.dot_general(qh.astype(BF16), khb, (((1,), (1,)), ((), ())),
                                   preferred_element_type=F32) * decay
            q_g = qh * eG
            ks = kh * jnp.exp(g_last - Gc)
            S = S_ref[h]
            Sb = S.astype(BF16)
            v_new = value - jnp.dot(k_cum.astype(BF16), Sb, preferred_element_type=F32)
            vnb = v_new.astype(BF16)
            o = (jnp.dot(q_g.astype(BF16), Sb, preferred_element_type=F32)
                 + jnp.dot(attn.astype(BF16), vnb, preferred_element_type=F32))
            S_ref[h] = S * jnp.exp(g_last) + lax.dot_general(
                ks.astype(BF16), vnb, (((0,), (0,)), ((), ())), preferred_element_type=F32)
            o = o * lax.rsqrt(jnp.mean(o * o, axis=-1, keepdims=True) + RMS_EPS) * nw
            o = o * _silu(zc[:, h * GDN_DV:(h + 1) * GDN_DV])
            o_ref[rows, h * GDN_DV:(h + 1) * GDN_DV] = o.astype(BF16)
        return carry

    lax.fori_loop(0, ncht, chunk, 0)


def _gdn(proj, small, cw, alog_pad, dtb_pad, nw, B, Tp, pad, tt, qkvcol, zcol):
    R = proj.shape[0]
    CQ = cw.shape[1]
    nT = Tp // tt
    lane_vec = pl.BlockSpec((1, LANES), lambda b, t: (0, 0))
    return pl.pallas_call(
        functools.partial(_gdn_kernel, pad=pad),
        out_shape=jax.ShapeDtypeStruct((R, GDN_V_DIM), BF16),
        grid=(B, nT),
        in_specs=[
            pl.BlockSpec((tt, CQ), lambda b, t: (b * nT + t, qkvcol)),
            pl.BlockSpec((tt, GDN_V_DIM), lambda b, t: (b * nT + t, zcol)),
            pl.BlockSpec((tt, LANES), lambda b, t: (b * nT + t, 0)),
            pl.BlockSpec((CONV_W, CQ), lambda b, t: (0, 0)),
            lane_vec, lane_vec, lane_vec,
        ],
        out_specs=pl.BlockSpec((tt, GDN_V_DIM), lambda b, t: (b * nT + t, 0)),
        scratch_shapes=[
            pltpu.VMEM((tt + SUBLANES, CQ), F32),
            pltpu.VMEM((tt, GDN_QK_DIM), F32),
            pltpu.VMEM((tt, GDN_QK_DIM), F32),
            pltpu.VMEM((tt, GDN_V_DIM), F32),
            pltpu.VMEM((tt, LANES), F32),
            pltpu.VMEM((tt, LANES), F32),
            pltpu.VMEM((GDN_H_V, GDN_DK, GDN_DV), F32),
        ],
        compiler_params=_params(("arbitrary", "arbitrary"), 48),
        name="gdn",
    )(proj, proj, small, cw, alog_pad, dtb_pad, nw)


def _merge_kernel(ya_ref, yb_ref, ga_ref, gb_ref, h_ref, wa_ref, wb_ref, wo_ref, g1_ref, b1_ref,
                  wr_ref, br_ref, hf_ref, h3_ref, idx_ref, gate_ref, cnt_ref, *, alpha):
    i = pl.program_id(0)
    tm, D = h_ref.shape
    ma = jnp.dot(ya_ref[...], wa_ref[...], preferred_element_type=F32)
    mb = jnp.dot(yb_ref[...], wb_ref[...], preferred_element_type=F32)
    mixed = _sigmoid(ga_ref[...]) * ma + _sigmoid(gb_ref[...]) * mb
    out = jnp.dot(mixed.astype(BF16), wo_ref[...], preferred_element_type=F32)
    hn = _layer_norm(alpha * h_ref[...] + out, g1_ref[...], b1_ref[...])
    hf_ref[...] = hn
    for s in range(D // LANES):
        h3_ref[:, s, :] = hn[:, s * LANES:(s + 1) * LANES]

    logits = jnp.dot(hn, wr_ref[...], precision=HIGHEST, preferred_element_type=F32) + br_ref[...]
    lane = lax.broadcasted_iota(I32, (tm, LANES), 1)
    vals = logits
    tops, sels = [], []
    idx_out = jnp.zeros((tm, LANES), I32)
    for j in range(TOP_K):
        m = jnp.max(vals, axis=-1, keepdims=True)
        idx = jnp.min(jnp.where(vals == m, lane, LANES), axis=-1, keepdims=True)
        sel = lane == idx
        tops.append(m)
        sels.append(sel)
        idx_out = jnp.where(lane == j, idx, idx_out)
        vals = jnp.where(sel, -jnp.inf, vals)
    es = [jnp.exp(m - tops[0]) for m in tops]
    inv = 1.0 / (es[0] + es[1] + es[2] + es[3])
    gate_out = jnp.zeros((tm, LANES), F32)
    onehot = jnp.zeros((tm, LANES), F32)
    for j in range(TOP_K):
        gate_out = jnp.where(lane == j, es[j] * inv, gate_out)
        onehot = onehot + sels[j].astype(F32)
    idx_ref[...] = idx_out
    gate_ref[...] = gate_out

    @pl.when(i == 0)
    def _():
        cnt_ref[...] = jnp.zeros(cnt_ref.shape, F32)

    cnt_ref[0:1, :] = cnt_ref[0:1, :] + jnp.sum(onehot, axis=0, keepdims=True)


def _merge(ya, yb, proj, h, wa, wb, wo, g1, b1, wr, br, alpha, tm, gacol, gbcol):
    R, D = h.shape
    C = ya.shape[1]
    S = D // LANES
    const2 = lambda i: (0, 0)
    single = pl.Buffered(1)
    return pl.pallas_call(
        functools.partial(_merge_kernel, alpha=alpha),
        out_shape=(
            jax.ShapeDtypeStruct((R, D), F32),
            jax.ShapeDtypeStruct((R, S, LANES), F32),
            jax.ShapeDtypeStruct((R, LANES), I32),
            jax.ShapeDtypeStruct((R, LANES), F32),
            jax.ShapeDtypeStruct((SUBLANES, LANES), F32),
        ),
        grid=(R // tm,),
        in_specs=[
            pl.BlockSpec((tm, C), lambda i: (i, 0)),
            pl.BlockSpec((tm, C), lambda i: (i, 0)),
            pl.BlockSpec((tm, D), lambda i: (i, gacol)),
            pl.BlockSpec((tm, D), lambda i: (i, gbcol)),
            pl.BlockSpec((tm, D), lambda i: (i, 0)),
            pl.BlockSpec((C, D), const2, pipeline_mode=single),
            pl.BlockSpec((C, D), const2, pipeline_mode=single),
            pl.BlockSpec((D, D), const2, pipeline_mode=single),
            pl.BlockSpec((1, D), const2),
            pl.BlockSpec((1, D), const2),
            pl.BlockSpec((D, LANES), const2, pipeline_mode=single),
            pl.BlockSpec((1, LANES), const2),
        ],
        out_specs=(
            pl.BlockSpec((tm, D), lambda i: (i, 0)),
            pl.BlockSpec((tm, S, LANES), lambda i: (i, 0, 0)),
            pl.BlockSpec((tm, LANES), lambda i: (i, 0)),
            pl.BlockSpec((tm, LANES), lambda i: (i, 0)),
            pl.BlockSpec((SUBLANES, LANES), const2),
        ),
        compiler_params=_params(("arbitrary",), 56),
        name="merge",
    )(ya, yb, proj, proj, h, wa, wb, wo, g1.reshape(1, D), b1.reshape(1, D), wr, br)


def _dest_kernel(idx_ref, start_ref, dest_ref, carry):
    i = pl.program_id(0)
    tr = idx_ref.shape[0]

    @pl.when(i == 0)
    def _():
        carry[...] = jnp.zeros(carry.shape, F32)

    idx = idx_ref[...]
    lane = lax.broadcasted_iota(I32, (tr, LANES), 1)
    ohs = [(lane == idx[:, j:j + 1]).astype(F32) for j in range(TOP_K)]
    oh = ohs[0] + ohs[1] + ohs[2] + ohs[3]
    ri = lax.broadcasted_iota(I32, (tr, tr), 0)
    ci = lax.broadcasted_iota(I32, (tr, tr), 1)
    stril = (ri > ci).astype(BF16)
    before = jnp.dot(stril, oh.astype(BF16), preferred_element_type=F32) + carry[0:1, :] + start_ref[...]
    dest = jnp.zeros((tr, LANES), I32)
    for j in range(TOP_K):
        pos = jnp.sum(ohs[j] * before, axis=-1, keepdims=True)
        dest = jnp.where(lane == j, pos.astype(I32), dest)
    dest_ref[...] = dest
    carry[0:1, :] = carry[0:1, :] + jnp.sum(oh, axis=0, keepdims=True)


def _dest(idx, starts, tr):
    R = idx.shape[0]
    return pl.pallas_call(
        _dest_kernel,
        out_shape=jax.ShapeDtypeStruct((R, LANES), I32),
        grid=(R // tr,),
        in_specs=[pl.BlockSpec((tr, LANES), lambda i: (i, 0)), pl.BlockSpec((1, LANES), lambda i: (0, 0))],
        out_specs=pl.BlockSpec((tr, LANES), lambda i: (i, 0)),
        scratch_shapes=[pltpu.VMEM((SUBLANES, LANES), F32)],
        compiler_params=_params(("arbitrary",), 32),
        name="moe_dest",
    )(idx, starts)


def _dispatch_kernel(dest_ref, x_ref, zero_ref, xb_ref, sem):
    del zero_ref
    tm = x_ref.shape[0]
    n = tm * TOP_K

    def copy(s):
        return pltpu.make_async_copy(x_ref.at[s // TOP_K], xb_ref.at[dest_ref[s]], sem)

    def start(s, c):
        copy(s).start()
        return c

    def wait(s, c):
        copy(s).wait()
        return c

    lax.fori_loop(0, n, start, 0)
    lax.fori_loop(0, n, wait, 0)


def _dispatch(dest_flat, h3, m_pad, tm):
    R, S, _ = h3.shape
    zeros = jnp.zeros((m_pad, S, LANES), F32)
    return pl.pallas_call(
        _dispatch_kernel,
        out_shape=jax.ShapeDtypeStruct((m_pad, S, LANES), F32),
        grid=(R // tm,),
        in_specs=[
            pl.BlockSpec((tm * TOP_K,), lambda i: (i,), memory_space=pltpu.SMEM),
            pl.BlockSpec((tm, S, LANES), lambda i: (i, 0, 0)),
            pl.BlockSpec(memory_space=pl.ANY),
        ],
        out_specs=pl.BlockSpec(memory_space=pl.ANY),
        scratch_shapes=[pltpu.SemaphoreType.DMA],
        input_output_aliases={2: 0},
        compiler_params=_params(("arbitrary",), 32),
        name="moe_dispatch",
    )(dest_flat, h3, zeros)


def _ffn_kernel(be_ref, x_ref, wg_ref, bg_ref, wl_ref, bl_ref, wd_ref, bd_ref, o_ref):
    del be_ref
    S = x_ref.shape[1]
    x = jnp.concatenate([x_ref[:, s, :] for s in range(S)], axis=1).astype(BF16)
    glu = jnp.dot(x, wg_ref[0], preferred_element_type=F32) + bg_ref[0]
    lin = jnp.dot(x, wl_ref[0], preferred_element_type=F32) + bl_ref[0]
    glu = jnp.minimum(glu, SWIGLU_LIMIT)
    lin = jnp.clip(lin, -SWIGLU_LIMIT, SWIGLU_LIMIT)
    act = glu * _sigmoid(SWIGLU_ALPHA * glu) * (lin + 1.0)
    y = jnp.dot(act.astype(BF16), wd_ref[0], preferred_element_type=F32) + bd_ref[0]
    for s in range(S):
        o_ref[:, s, :] = y[:, s * LANES:(s + 1) * LANES]


def _ffn(block_exp, xb, wg, bg, wl, bl, wd, bd, tm):
    m_pad, S, _ = xb.shape
    E, D, DE = wg.shape
    n_blocks = m_pad // tm
    grid_spec = pltpu.PrefetchScalarGridSpec(
        num_scalar_prefetch=1,
        grid=(n_blocks,),
        in_specs=[
            pl.BlockSpec((tm, S, LANES), lambda b, be: (b, 0, 0)),
            pl.BlockSpec((1, D, DE), lambda b, be: (be[b], 0, 0)),
            pl.BlockSpec((1, 1, DE), lambda b, be: (be[b], 0, 0)),
            pl.BlockSpec((1, D, DE), lambda b, be: (be[b], 0, 0)),
            pl.BlockSpec((1, 1, DE), lambda b, be: (be[b], 0, 0)),
            pl.BlockSpec((1, DE, D), lambda b, be: (be[b], 0, 0)),
            pl.BlockSpec((1, 1, D), lambda b, be: (be[b], 0, 0)),
        ],
        out_specs=pl.BlockSpec((tm, S, LANES), lambda b, be: (b, 0, 0)),
    )
    return pl.pallas_call(
        _ffn_kernel,
        out_shape=jax.ShapeDtypeStruct((m_pad, S, LANES), F32),
        grid_spec=grid_spec,
        compiler_params=_params(("arbitrary",), 56),
        name="moe_ffn",
    )(block_exp, xb, wg, bg, wl, bl, wd, bd)


def _combine_kernel(dest_ref, gate_ref, h_ref, g2_ref, b2_ref, yb_ref, of_ref, ob_ref, buf, sem, *, alpha):
    tm, D = h_ref.shape
    S = D // LANES
    n = tm * TOP_K

    def copy(s):
        return pltpu.make_async_copy(yb_ref.at[dest_ref[s]], buf.at[s % TOP_K, s // TOP_K], sem)

    def start(s, c):
        copy(s).start()
        return c

    def wait(s, c):
        copy(s).wait()
        return c

    lax.fori_loop(0, n, start, 0)
    lax.fori_loop(0, n, wait, 0)

    gate = gate_ref[...]
    moe = None
    for j in range(TOP_K):
        yj = jnp.concatenate([buf[j, :, s, :] for s in range(S)], axis=1)
        term = gate[:, j:j + 1] * yj
        moe = term if moe is None else moe + term
    y = _layer_norm(alpha * h_ref[...] + moe, g2_ref[...], b2_ref[...])
    of_ref[...] = y
    ob_ref[...] = y.astype(BF16)


def _combine(dest_flat, gate, h, g2, b2, yb, alpha, tm):
    R, D = h.shape
    S = D // LANES
    row = pl.BlockSpec((tm, D), lambda i: (i, 0))
    vec = pl.BlockSpec((1, D), lambda i: (0, 0))
    return pl.pallas_call(
        functools.partial(_combine_kernel, alpha=alpha),
        out_shape=(jax.ShapeDtypeStruct((R, D), F32), jax.ShapeDtypeStruct((R, D), BF16)),
        grid=(R // tm,),
        in_specs=[
            pl.BlockSpec((tm * TOP_K,), lambda i: (i,), memory_space=pltpu.SMEM),
            pl.BlockSpec((tm, LANES), lambda i: (i, 0)),
            row, vec, vec,
            pl.BlockSpec(memory_space=pl.ANY),
        ],
        out_specs=(row, row),
        scratch_shapes=[pltpu.VMEM((TOP_K, tm, S, LANES), F32), pltpu.SemaphoreType.DMA],
        compiler_params=_params(("arbitrary",), 48),
        name="moe_combine",
    )(dest_flat, gate, h, g2.reshape(1, D), b2.reshape(1, D), yb)


def _lane_pad(v, offset):
    return jnp.zeros((1, LANES), F32).at[0, offset:offset + v.shape[0]].set(v.astype(F32))


def kernel(x, meta_tokens, ln_in_g, ln_in_b, w_in, rg_conv_w, rg_conv_b, rg_wx, rg_bx, rg_wa, rg_ba, rg_lambda, gdn_conv_w, gdn_a_log, gdn_dt_bias, gdn_norm_w, w_branch_a, w_branch_b, w_out, ln1_g, ln1_b, w_router, b_router, w_gate_up, b_gate_up, w_down, b_down, ln2_g, ln2_b):
    B, seq, D = x.shape
    depth = w_in.shape[0]
    E = w_router.shape[2]
    DE = w_down.shape[2]
    d_rnn = D // 2
    assert D == 2 * GDN_V_DIM and d_rnn == GDN_V_DIM and E <= LANES
    alpha = float((2 * depth) ** 0.25)

    T = N_META + seq
    pad = (-T) % CHUNK
    Tp = T + pad
    R = B * Tp

    tm_ln = _pick_tile(R, 512)
    tm_in = _pick_tile(R, 1040)
    tn_in = 1024
    tt_rg = _pick_tile(Tp, 416, SUBLANES)
    tt_gdn = _pick_tile(Tp, 320, CHUNK)
    tm_mg = _pick_tile(R, 256)
    tm_moe = _pick_tile(R, 256)
    n_blocks = -(-(R * TOP_K) // tm_moe) + E
    m_pad = n_blocks * tm_moe

    meta = jnp.broadcast_to(meta_tokens[None].astype(x.dtype), (B, N_META, D))
    h0 = jnp.concatenate([jnp.zeros((B, pad, D), x.dtype), meta, x], axis=1).reshape(R, D)
    h, hb = _ln_in(h0, ln_in_g, ln_in_b, tm_ln)

    o_rx, o_ry = 0, d_rnn
    o_q = 2 * d_rnn
    o_z = o_q + 2 * GDN_QK_DIM + GDN_V_DIM
    o_beta = o_z + GDN_V_DIM
    o_dec = o_beta + GDN_H_V
    o_ga = o_dec + GDN_H_V
    o_gb = o_ga + D

    for l in range(depth):
        wl = w_in[l]
        w_main = jnp.concatenate(
            [wl[:, o_ga:o_ga + D], wl[:, o_gb:o_gb + D], wl[:, o_q:o_z], wl[:, o_rx:o_q], wl[:, o_z:o_beta]],
            axis=1).astype(BF16)
        w_small = jnp.zeros((D, LANES), F32).at[:, :2 * GDN_H_V].set(wl[:, o_beta:o_ga]).astype(BF16)
        proj, small = _inproj(hb, w_main, w_small, tm_in, tn_in)

        ya = _rglru(proj, rg_conv_w[l], rg_conv_b[l], rg_wx[l].astype(BF16), rg_bx[l], rg_wa[l].astype(BF16),
                    rg_ba[l], rg_lambda[l], B, Tp, pad, tt_rg, xcol=(2 * D + 2048) // d_rnn,
                    ycol=(2 * D + 2048) // d_rnn + 1)
        yb = _gdn(proj, small, gdn_conv_w[l], _lane_pad(gdn_a_log[l], GDN_H_V), _lane_pad(gdn_dt_bias[l], GDN_H_V),
                  gdn_norm_w[l].reshape(1, GDN_DV).astype(F32), B, Tp, pad, tt_gdn, qkvcol=2,
                  zcol=(2 * D + 2048 + 2 * d_rnn) // GDN_V_DIM)

        wr = jnp.zeros((D, LANES), F32).at[:, :E].set(w_router[l])
        br = jnp.full((1, LANES), -jnp.inf, F32).at[0, :E].set(b_router[l])
        hn, h3, idx, gate, counts = _merge(
            ya, yb, proj, h, w_branch_a[l].astype(BF16), w_branch_b[l].astype(BF16), w_out[l].astype(BF16),
            ln1_g[l], ln1_b[l], wr, br, alpha, tm_mg, gacol=0, gbcol=1)

        cnt = counts[0, :E].astype(I32)
        padded = (cnt + tm_moe - 1) // tm_moe * tm_moe
        pad_ends = jnp.cumsum(padded)
        starts = _lane_pad(pad_ends - padded, 0)
        block_exp = jnp.minimum(
            jnp.searchsorted(pad_ends, jnp.arange(n_blocks, dtype=I32) * tm_moe, side='right'), E - 1).astype(I32)
        dest = _dest(idx, starts, tm_moe)
        dest_flat = dest[:, :TOP_K].reshape(R * TOP_K)

        xb = _dispatch(dest_flat, h3, m_pad, tm_moe)
        wgu = w_gate_up[l]
        bgu = b_gate_up[l]
        ybk = _ffn(block_exp, xb,
                   wgu[:, :, 0::2].astype(BF16), bgu[:, None, 0::2],
                   wgu[:, :, 1::2].astype(BF16), bgu[:, None, 1::2],
                   w_down[l].astype(BF16), b_down[l][:, None, :], tm_moe)
        h, hb = _combine(dest_flat, gate, hn, ln2_g[l], ln2_b[l], ybk, alpha, tm_moe)

    return h.reshape(B, Tp, D)[:, pad + N_META:]
```

```python
import functools

import jax
import jax.numpy as jnp
from jax import lax
from jax.experimental import pallas as pl
from jax.experimental.pallas import tpu as pltpu

F32 = jnp.float32
BF16 = jnp.bfloat16
I32 = jnp.int32
U32 = jnp.uint32

N_META = 16
CONV_W = 4
CHUNK = 64
TOP_K = 4
RG_BLOCKS = 8
RG_C = 8.0
GDN_DK = 128
GDN_DV = 128
GDN_H_QK = 4
GDN_H_V = 8
GDN_QK_DIM = GDN_H_QK * GDN_DK
GDN_V_DIM = GDN_H_V * GDN_DV
SWIGLU_LIMIT = 7.0
SWIGLU_ALPHA = 1.702
LN_EPS = 1e-5
RMS_EPS = 1e-6

LANES = 128
SUBLANES = 8
MIB = 1024 * 1024
HIGHEST = lax.Precision.HIGHEST


def _pick_tile(n, target, mult=16):
    best = None
    for t in range(mult, min(n, target) + 1, mult):
        if n % t == 0:
            best = t
    assert best is not None, (n, target, mult)
    return best


def _params(sem, vmem_mib):
    return pltpu.CompilerParams(dimension_semantics=sem, vmem_limit_bytes=vmem_mib * MIB)


def _layer_norm(x, g, b):
    mu = jnp.mean(x, axis=-1, keepdims=True)
    xc = x - mu
    var = jnp.mean(xc * xc, axis=-1, keepdims=True)
    return xc * lax.rsqrt(var + LN_EPS) * g + b


def _sigmoid(x):
    return 1.0 / (1.0 + jnp.exp(-x))


def _softplus(x):
    return jnp.maximum(x, 0.0) + jnp.log1p(jnp.exp(-jnp.abs(x)))


def _silu(x):
    return x * _sigmoid(x)


def _gelu_tanh(x):
    return 0.5 * x * (1.0 + jnp.tanh(0.7978845608028654 * (x + 0.044715 * x * x * x)))


def _pack_pairs(x):
    n = x.shape[1] // 2
    xb = x.astype(BF16).astype(F32)
    lo = lax.bitcast_convert_type(xb[:, :n], U32) >> 16
    hi = lax.bitcast_convert_type(xb[:, n:], U32) & jnp.uint32(0xFFFF0000)
    return lo | hi


def _unpack_pairs(w):
    lo = lax.bitcast_convert_type(w << 16, F32)
    hi = lax.bitcast_convert_type(w & jnp.uint32(0xFFFF0000), F32)
    return jnp.concatenate([lo, hi], axis=1).astype(BF16)


def _ln_in_kernel(x_ref, g_ref, b_ref, of_ref, ob_ref):
    y = _layer_norm(x_ref[...], g_ref[...], b_ref[...])
    of_ref[...] = y
    ob_ref[...] = y.astype(BF16)


def _ln_in(x2d, g, b, tm):
    R, D = x2d.shape
    row = pl.BlockSpec((tm, D), lambda i: (i, 0))
    vec = pl.BlockSpec((1, D), lambda i: (0, 0))
    return pl.pallas_call(
        _ln_in_kernel,
        out_shape=(jax.ShapeDtypeStruct((R, D), F32), jax.ShapeDtypeStruct((R, D), BF16)),
        grid=(R // tm,),
        in_specs=[row, vec, vec],
        out_specs=(row, row),
        compiler_params=_params(("arbitrary",), 32),
        name="ln_in",
    )(x2d, g.reshape(1, D), b.reshape(1, D))


def _inproj_kernel(x_ref, w_ref, ws_ref, o_ref, os_ref):
    x = x_ref[...]
    o_ref[...] = jnp.dot(x, w_ref[...], preferred_element_type=F32)

    @pl.when(pl.program_id(1) == 0)
    def _():
        os_ref[...] = jnp.dot(x, ws_ref[...], preferred_element_type=F32)


def _inproj(hb, w_main, w_small, tm, tn):
    R, D = hb.shape
    N = w_main.shape[1]
    return pl.pallas_call(
        _inproj_kernel,
        out_shape=(jax.ShapeDtypeStruct((R, N), F32), jax.ShapeDtypeStruct((R, LANES), F32)),
        grid=(R // tm, N // tn),
        in_specs=[
            pl.BlockSpec((tm, D), lambda i, j: (i, 0)),
            pl.BlockSpec((D, tn), lambda i, j: (0, j)),
            pl.BlockSpec((D, LANES), lambda i, j: (0, 0)),
        ],
        out_specs=(
            pl.BlockSpec((tm, tn), lambda i, j: (i, j)),
            pl.BlockSpec((tm, LANES), lambda i, j: (i, 0)),
        ),
        compiler_params=_params(("arbitrary", "arbitrary"), 48),
        name="inproj",
    )(hb, w_main, w_small)


def _causal_conv(x, xbuf, cw_ref, tt):
    xbuf[SUBLANES:SUBLANES + tt, :] = x
    y = cw_ref[CONV_W - 1:CONV_W, :] * x
    for j in range(CONV_W - 1):
        off = SUBLANES - (CONV_W - 1) + j
        y = y + cw_ref[j:j + 1, :] * xbuf[off:off + tt, :]
    xbuf[0:SUBLANES, :] = xbuf[tt:tt + SUBLANES, :]
    return y


def _rglru_kernel(x_ref, y_ref, cw_ref, cb_ref, wx_ref, bx_ref, wa_ref, ba_ref, lam_ref,
                  o_ref, xbuf, a_s, u_s, hc, *, pad):
    t = pl.program_id(1)
    tt, C = x_ref.shape
    bs = C // RG_BLOCKS

    @pl.when(t == 0)
    def _():
        xbuf[0:SUBLANES, :] = jnp.zeros((SUBLANES, C), F32)
        hc[...] = jnp.zeros((SUBLANES, C), F32)

    row = lax.broadcasted_iota(I32, (tt, 1), 0)
    valid = (row + t * tt) >= pad
    x = jnp.where(valid, x_ref[...], 0.0)
    xc = _causal_conv(x, xbuf, cw_ref, tt) + cb_ref[...]

    xcb = xc.astype(BF16)
    gi = jnp.concatenate(
        [jnp.dot(xcb[:, h * bs:(h + 1) * bs], wx_ref[h], preferred_element_type=F32) for h in range(RG_BLOCKS)],
        axis=1)
    gr = jnp.concatenate(
        [jnp.dot(xcb[:, h * bs:(h + 1) * bs], wa_ref[h], preferred_element_type=F32) for h in range(RG_BLOCKS)],
        axis=1)
    gi = _sigmoid(gi + bx_ref[...])
    gr = _sigmoid(gr + ba_ref[...])
    log_a = (-RG_C) * gr * _softplus(-lam_ref[...])
    a = jnp.exp(log_a)
    u = jnp.sqrt(-jnp.tanh(log_a) * (a * a + 1.0)) * gi * xc
    u = jnp.where(valid, u, 0.0)

    rowmod = row % SUBLANES
    for d in (1, 2, 4):
        m = rowmod >= d
        a_sh = pltpu.roll(a, d, 0)
        u_sh = pltpu.roll(u, d, 0)
        u = jnp.where(m, a * u_sh + u, u)
        a = jnp.where(m, a * a_sh, a)
    a_s[...] = a
    u_s[...] = u

    def group(g, hprev):
        r0 = pl.multiple_of(g * SUBLANES, SUBLANES)
        h = a_s[pl.ds(r0, SUBLANES), :] * hprev + u_s[pl.ds(r0, SUBLANES), :]
        u_s[pl.ds(r0, SUBLANES), :] = h
        return jnp.broadcast_to(h[SUBLANES - 1:SUBLANES, :], (SUBLANES, C))

    hc[...] = lax.fori_loop(0, tt // SUBLANES, group, hc[...])
    o_ref[...] = (u_s[...] * _gelu_tanh(y_ref[...])).astype(BF16)


def _rglru(proj, cw, cb, wx, bx, wa, ba, lam, B, Tp, pad, tt, xcol, ycol):
    R = proj.shape[0]
    C = cw.shape[1]
    nT = Tp // tt
    vec = pl.BlockSpec((1, C), lambda b, t: (0, 0))
    wspec = pl.BlockSpec(wx.shape, lambda b, t: (0, 0, 0))
    return pl.pallas_call(
        functools.partial(_rglru_kernel, pad=pad),
        out_shape=jax.ShapeDtypeStruct((R, C), BF16),
        grid=(B, nT),
        in_specs=[
            pl.BlockSpec((tt, C), lambda b, t: (b * nT + t, xcol)),
            pl.BlockSpec((tt, C), lambda b, t: (b * nT + t, ycol)),
            pl.BlockSpec((CONV_W, C), lambda b, t: (0, 0)),
            vec, wspec, vec, wspec, vec, vec,
        ],
        out_specs=pl.BlockSpec((tt, C), lambda b, t: (b * nT + t, 0)),
        scratch_shapes=[
            pltpu.VMEM((tt + SUBLANES, C), F32),
            pltpu.VMEM((tt, C), F32),
            pltpu.VMEM((tt, C), F32),
            pltpu.VMEM((SUBLANES, C), F32),
        ],
        compiler_params=_params(("arbitrary", "arbitrary"), 48),
        name="rglru",
    )(proj, proj, cw, cb.reshape(1, C), wx, bx.reshape(1, C), wa, ba.reshape(1, C), lam.reshape(1, C))


def _bdot(a, b):
    return jnp.dot(a.astype(BF16), b.astype(BF16), preferred_element_type=F32)


def _bdot_nt(a, b):
    return lax.dot_general(a.astype(BF16), b.astype(BF16), (((1,), (1,)), ((), ())), preferred_element_type=F32)


def _bdot_tn(a, b):
    return lax.dot_general(a.astype(BF16), b.astype(BF16), (((0,), (0,)), ((), ())), preferred_element_type=F32)


def _unit_lower_inverses(Ls, masks):
    eye = masks[0]
    dinvs = [eye - L * masks[1] for L in Ls]
    for m in masks[2:]:
        ts = [_bdot(d, L * m) for d, L in zip(dinvs, Ls)]
        dinvs = [d - _bdot(t, d) for d, t in zip(dinvs, ts)]
    return dinvs


def _gdn_kernel(qkv_ref, z_ref, sm_ref, cw_ref, alog_ref, dtb_ref, nw_ref,
                o_ref, xbuf, q_s, k_s, v_s, b_s, g_s, S_ref, *, pad):
    t = pl.program_id(1)
    tt, CQ = qkv_ref.shape
    ncht = tt // CHUNK
    rep = GDN_H_V // GDN_H_QK

    @pl.when(t == 0)
    def _():
        xbuf[0:SUBLANES, :] = jnp.zeros((SUBLANES, CQ), F32)
        S_ref[...] = jnp.zeros(S_ref.shape, F32)

    row = lax.broadcasted_iota(I32, (tt, 1), 0)
    valid = (row + t * tt) >= pad
    x = jnp.where(valid, qkv_ref[...], 0.0)
    xc = _silu(_causal_conv(x, xbuf, cw_ref, tt))

    for hq in range(GDN_H_QK):
        qh = xc[:, hq * GDN_DK:(hq + 1) * GDN_DK]
        kh = xc[:, GDN_QK_DIM + hq * GDN_DK:GDN_QK_DIM + (hq + 1) * GDN_DK]
        qn = qh * lax.rsqrt(jnp.sum(qh * qh, axis=-1, keepdims=True) + RMS_EPS) * (GDN_DK ** -0.5)
        kn = kh * lax.rsqrt(jnp.sum(kh * kh, axis=-1, keepdims=True) + RMS_EPS)
        q_s[:, hq * GDN_DK:(hq + 1) * GDN_DK] = qn
        k_s[:, hq * GDN_DK:(hq + 1) * GDN_DK] = kn
    v_s[...] = xc[:, 2 * GDN_QK_DIM:]

    sm = sm_ref[...]
    b_s[...] = _sigmoid(sm)
    g = -jnp.exp(alog_ref[...]) * _softplus(sm + dtb_ref[...])
    g_s[...] = jnp.where(valid, g, 0.0)

    ri = lax.broadcasted_iota(I32, (CHUNK, CHUNK), 0)
    ci = lax.broadcasted_iota(I32, (CHUNK, CHUNK), 1)
    causal = ri >= ci
    strict = ri > ci
    tril = causal.astype(F32)
    masks = [(ri == ci).astype(F32)]
    s = 1
    while s < CHUNK:
        masks.append(((ri // (2 * s) == ci // (2 * s)) & (ri % (2 * s) >= s) & (ci % (2 * s) < s)).astype(F32))
        s *= 2
    nw = nw_ref[...]

    def chunk(c, carry):
        r0 = pl.multiple_of(c * CHUNK, CHUNK)
        rows = pl.ds(r0, CHUNK)
        G = jnp.dot(tril, g_s[rows, :], precision=HIGHEST, preferred_element_type=F32)
        GT = G.T
        bet = b_s[rows, :]
        zc = z_ref[rows, :]
        heads = range(GDN_H_V)
        qs = [q_s[rows, hq * GDN_DK:(hq + 1) * GDN_DK] for hq in range(GDN_H_QK)]
        ks_ = [k_s[rows, hq * GDN_DK:(hq + 1) * GDN_DK] for hq in range(GDN_H_QK)]
        kk = [_bdot_nt(k, k) for k in ks_]
        qk = [_bdot_nt(q, k) for q, k in zip(qs, ks_)]
        bcol = [bet[:, h:h + 1] for h in heads]
        Gc = [G[:, GDN_H_V + h:GDN_H_V + h + 1] for h in heads]
        decay = [jnp.where(causal, jnp.exp(jnp.minimum(Gc[h] - GT[GDN_H_V + h:GDN_H_V + h + 1, :], 0.0)), 0.0)
                 for h in heads]
        eG = [jnp.exp(Gc[h]) for h in heads]
        g_last = [Gc[h][CHUNK - 1:CHUNK, :] for h in heads]
        Ls = [jnp.where(strict, kk[h // rep] * bcol[h] * decay[h], 0.0) for h in heads]
        Ts = _unit_lower_inverses(Ls, masks)
        sols = [_bdot(Ts[h], jnp.concatenate(
            [v_s[rows, h * GDN_DV:(h + 1) * GDN_DV] * bcol[h], ks_[h // rep] * (bcol[h] * eG[h])], axis=1))
            for h in heads]
        Sold = [S_ref[h] for h in heads]
        PS = [_bdot(jnp.concatenate([sols[h][:, GDN_DV:], qs[h // rep] * eG[h]], axis=0), Sold[h]) for h in heads]
        v_new = [sols[h][:, :GDN_DV] - PS[h][:CHUNK] for h in heads]
        o = [PS[h][CHUNK:] + _bdot(qk[h // rep] * decay[h], v_new[h]) for h in heads]
        for h in heads:
            S_ref[h] = Sold[h] * jnp.exp(g_last[h]) + _bdot_tn(
                ks_[h // rep] * jnp.exp(g_last[h] - Gc[h]), v_new[h])
        for h in heads:
            oh = o[h] * lax.rsqrt(jnp.mean(o[h] * o[h], axis=-1, keepdims=True) + RMS_EPS) * nw
            oh = oh * _silu(zc[:, h * GDN_DV:(h + 1) * GDN_DV])
            o_ref[rows, h * GDN_DV:(h + 1) * GDN_DV] = oh.astype(BF16)
        return carry

    lax.fori_loop(0, ncht, chunk, 0)


def _gdn(proj, small, cw, alog_pad, dtb_pad, nw, B, Tp, pad, tt, qkvcol, zcol):
    R = proj.shape[0]
    CQ = cw.shape[1]
    nT = Tp // tt
    lane_vec = pl.BlockSpec((1, LANES), lambda b, t: (0, 0))
    return pl.pallas_call(
        functools.partial(_gdn_kernel, pad=pad),
        out_shape=jax.ShapeDtypeStruct((R, GDN_V_DIM), BF16),
        grid=(B, nT),
        in_specs=[
            pl.BlockSpec((tt, CQ), lambda b, t: (b * nT + t, qkvcol)),
            pl.BlockSpec((tt, GDN_V_DIM), lambda b, t: (b * nT + t, zcol)),
            pl.BlockSpec((tt, LANES), lambda b, t: (b * nT + t, 0)),
            pl.BlockSpec((CONV_W, CQ), lambda b, t: (0, 0)),
            lane_vec, lane_vec, lane_vec,
        ],
        out_specs=pl.BlockSpec((tt, GDN_V_DIM), lambda b, t: (b * nT + t, 0)),
        scratch_shapes=[
            pltpu.VMEM((tt + SUBLANES, CQ), F32),
            pltpu.VMEM((tt, GDN_QK_DIM), F32),
            pltpu.VMEM((tt, GDN_QK_DIM), F32),
            pltpu.VMEM((tt, GDN_V_DIM), F32),
            pltpu.VMEM((tt, LANES), F32),
            pltpu.VMEM((tt, LANES), F32),
            pltpu.VMEM((GDN_H_V, GDN_DK, GDN_DV), F32),
        ],
        compiler_params=_params(("arbitrary", "arbitrary"), 48),
        name="gdn",
    )(proj, proj, small, cw, alog_pad, dtb_pad, nw)


def _merge_kernel(ya_ref, yb_ref, ga_ref, gb_ref, h_ref, wa_ref, wb_ref, wo_ref, g1_ref, b1_ref,
                  wrh_ref, wrl_ref, br_ref, hf_ref, hp_ref, idx_ref, gate_ref, cnt_ref, *, alpha):
    i = pl.program_id(0)
    tm, D = h_ref.shape
    ma = jnp.dot(ya_ref[...], wa_ref[...], preferred_element_type=F32)
    mb = jnp.dot(yb_ref[...], wb_ref[...], preferred_element_type=F32)
    mixed = _sigmoid(ga_ref[...]) * ma + _sigmoid(gb_ref[...]) * mb
    out = jnp.dot(mixed.astype(BF16), wo_ref[...], preferred_element_type=F32)
    hn = _layer_norm(alpha * h_ref[...] + out, g1_ref[...], b1_ref[...])
    hf_ref[...] = hn
    hp_ref[...] = _pack_pairs(hn)

    hn_hi = hn.astype(BF16)
    hn_lo = (hn - hn_hi.astype(F32)).astype(BF16)
    logits = (jnp.dot(hn_hi, wrh_ref[...], preferred_element_type=F32)
              + jnp.dot(hn_lo, wrh_ref[...], preferred_element_type=F32)
              + jnp.dot(hn_hi, wrl_ref[...], preferred_element_type=F32)) + br_ref[...]
    lane = lax.broadcasted_iota(I32, (tm, LANES), 1)
    vals = logits
    tops, sels = [], []
    idx_out = jnp.zeros((tm, LANES), I32)
    for j in range(TOP_K):
        m = jnp.max(vals, axis=-1, keepdims=True)
        idx = jnp.min(jnp.where(vals == m, lane, LANES), axis=-1, keepdims=True)
        sel = lane == idx
        tops.append(m)
        sels.append(sel)
        idx_out = jnp.where(lane == j, idx, idx_out)
        vals = jnp.where(sel, -jnp.inf, vals)
    es = [jnp.exp(m - tops[0]) for m in tops]
    inv = 1.0 / (es[0] + es[1] + es[2] + es[3])
    gate_out = jnp.zeros((tm, LANES), F32)
    onehot = jnp.zeros((tm, LANES), F32)
    for j in range(TOP_K):
        gate_out = jnp.where(lane == j, es[j] * inv, gate_out)
        onehot = onehot + sels[j].astype(F32)
    idx_ref[...] = idx_out
    gate_ref[...] = gate_out

    @pl.when(i == 0)
    def _():
        cnt_ref[...] = jnp.zeros(cnt_ref.shape, F32)

    cnt_ref[0:1, :] = cnt_ref[0:1, :] + jnp.sum(onehot, axis=0, keepdims=True)


def _merge(ya, yb, proj, h, wa, wb, wo, g1, b1, wr, br, alpha, tm, gacol, gbcol):
    R, D = h.shape
    C = ya.shape[1]
    const2 = lambda i: (0, 0)
    single = pl.Buffered(1)
    wr_hi = wr.astype(BF16)
    wr_lo = (wr - wr_hi.astype(F32)).astype(BF16)
    return pl.pallas_call(
        functools.partial(_merge_kernel, alpha=alpha),
        out_shape=(
            jax.ShapeDtypeStruct((R, D), F32),
            jax.ShapeDtypeStruct((R, D // 2), U32),
            jax.ShapeDtypeStruct((R, LANES), I32),
            jax.ShapeDtypeStruct((R, LANES), F32),
            jax.ShapeDtypeStruct((SUBLANES, LANES), F32),
        ),
        grid=(R // tm,),
        in_specs=[
            pl.BlockSpec((tm, C), lambda i: (i, 0)),
            pl.BlockSpec((tm, C), lambda i: (i, 0)),
            pl.BlockSpec((tm, D), lambda i: (i, gacol)),
            pl.BlockSpec((tm, D), lambda i: (i, gbcol)),
            pl.BlockSpec((tm, D), lambda i: (i, 0)),
            pl.BlockSpec((C, D), const2, pipeline_mode=single),
            pl.BlockSpec((C, D), const2, pipeline_mode=single),
            pl.BlockSpec((D, D), const2, pipeline_mode=single),
            pl.BlockSpec((1, D), const2),
            pl.BlockSpec((1, D), const2),
            pl.BlockSpec((D, LANES), const2, pipeline_mode=single),
            pl.BlockSpec((D, LANES), const2, pipeline_mode=single),
            pl.BlockSpec((1, LANES), const2),
        ],
        out_specs=(
            pl.BlockSpec((tm, D), lambda i: (i, 0)),
            pl.BlockSpec((tm, D // 2), lambda i: (i, 0)),
            pl.BlockSpec((tm, LANES), lambda i: (i, 0)),
            pl.BlockSpec((tm, LANES), lambda i: (i, 0)),
            pl.BlockSpec((SUBLANES, LANES), const2),
        ),
        compiler_params=_params(("arbitrary",), 56),
        name="merge",
    )(ya, yb, proj, proj, h, wa, wb, wo, g1.reshape(1, D), b1.reshape(1, D), wr_hi, wr_lo, br)


def _dest_kernel(idx_ref, start_ref, dest_ref, carry):
    i = pl.program_id(0)
    tr = idx_ref.shape[0]

    @pl.when(i == 0)
    def _():
        carry[...] = jnp.zeros(carry.shape, F32)

    idx = idx_ref[...]
    lane = lax.broadcasted_iota(I32, (tr, LANES), 1)
    ohs = [(lane == idx[:, j:j + 1]).astype(F32) for j in range(TOP_K)]
    oh = ohs[0] + ohs[1] + ohs[2] + ohs[3]
    ri = lax.broadcasted_iota(I32, (tr, tr), 0)
    ci = lax.broadcasted_iota(I32, (tr, tr), 1)
    stril = (ri > ci).astype(BF16)
    before = jnp.dot(stril, oh.astype(BF16), preferred_element_type=F32) + carry[0:1, :] + start_ref[...]
    dest = jnp.zeros((tr, LANES), I32)
    for j in range(TOP_K):
        pos = jnp.sum(ohs[j] * before, axis=-1, keepdims=True)
        dest = jnp.where(lane == j, pos.astype(I32), dest)
    dest_ref[...] = dest
    carry[0:1, :] = carry[0:1, :] + jnp.sum(oh, axis=0, keepdims=True)


def _dest(idx, starts, tr):
    R = idx.shape[0]
    return pl.pallas_call(
        _dest_kernel,
        out_shape=jax.ShapeDtypeStruct((R, LANES), I32),
        grid=(R // tr,),
        in_specs=[pl.BlockSpec((tr, LANES), lambda i: (i, 0)), pl.BlockSpec((1, LANES), lambda i: (0, 0))],
        out_specs=pl.BlockSpec((tr, LANES), lambda i: (i, 0)),
        scratch_shapes=[pltpu.VMEM((SUBLANES, LANES), F32)],
        compiler_params=_params(("arbitrary",), 32),
        name="moe_dest",
    )(idx, starts)


ROW_UNROLL = 2


def _dispatch_kernel(nv_ref, dest_ref, x_ref, xb_ref, zbuf, sem, zsem):
    tm = x_ref.shape[0]
    n_blocks = xb_ref.shape[0] // tm

    @pl.when(pl.program_id(0) == 0)
    def _():
        zbuf[...] = jnp.zeros(zbuf.shape, U32)

        def zero_copy(b):
            return pltpu.make_async_copy(zbuf, xb_ref.at[pl.ds(pl.multiple_of(b * tm, tm), tm), :], zsem)

        def start(b, c):
            @pl.when(nv_ref[b] < tm)
            def _():
                zero_copy(b).start()
            return c

        def wait(b, c):
            @pl.when(nv_ref[b] < tm)
            def _():
                zero_copy(b).wait()
            return c

        lax.fori_loop(0, n_blocks, start, 0)
        lax.fori_loop(0, n_blocks, wait, 0)

    def row(r, c):
        for j in range(TOP_K):
            d = dest_ref[r * TOP_K + j]
            pltpu.make_async_copy(x_ref.at[pl.ds(r, 1), :], xb_ref.at[pl.ds(d, 1), :], sem).start()
        return c

    lax.fori_loop(0, tm, row, 0, unroll=ROW_UNROLL)
    for j in range(TOP_K):
        pltpu.make_async_copy(x_ref, xb_ref.at[pl.ds(0, tm), :], sem).wait()


def _dispatch(n_valid, dest_flat, hp, m_pad, tm):
    R, W = hp.shape
    grid_spec = pltpu.PrefetchScalarGridSpec(
        num_scalar_prefetch=1,
        grid=(R // tm,),
        in_specs=[
            pl.BlockSpec((tm * TOP_K,), lambda i, nv: (i,), memory_space=pltpu.SMEM),
            pl.BlockSpec((tm, W), lambda i, nv: (i, 0)),
        ],
        out_specs=pl.BlockSpec(memory_space=pl.ANY),
        scratch_shapes=[pltpu.VMEM((tm, W), U32), pltpu.SemaphoreType.DMA, pltpu.SemaphoreType.DMA],
    )
    return pl.pallas_call(
        _dispatch_kernel,
        out_shape=jax.ShapeDtypeStruct((m_pad, W), U32),
        grid_spec=grid_spec,
        compiler_params=_params(("arbitrary",), 32),
        name="moe_dispatch",
    )(n_valid, dest_flat, hp)


def _deint_kernel(w_ref, p_ref, g_ref, l_ref):
    w = w_ref[0].astype(BF16)
    perm = p_ref[...]
    width = perm.shape[0]
    half = width // 2
    for g in range(w.shape[1] // width):
        t = jnp.dot(w[:, g * width:(g + 1) * width], perm, preferred_element_type=F32)
        g_ref[0, :, g * half:(g + 1) * half] = t[:, :half].astype(BF16)
        l_ref[0, :, g * half:(g + 1) * half] = t[:, half:].astype(BF16)


def _deinterleave(w_gu, tr):
    N, D, DE2 = w_gu.shape
    DE = DE2 // 2
    width = 2 * LANES
    src = lax.broadcasted_iota(I32, (width, width), 0)
    dst = lax.broadcasted_iota(I32, (width, width), 1)
    perm = (src == jnp.where(dst < LANES, 2 * dst, 2 * (dst - LANES) + 1)).astype(BF16)
    out = jax.ShapeDtypeStruct((N, D, DE), BF16)
    return pl.pallas_call(
        _deint_kernel,
        out_shape=(out, out),
        grid=(N, D // tr),
        in_specs=[
            pl.BlockSpec((1, tr, DE2), lambda n, i: (n, i, 0)),
            pl.BlockSpec((width, width), lambda n, i: (0, 0)),
        ],
        out_specs=(pl.BlockSpec((1, tr, DE), lambda n, i: (n, i, 0)),
                   pl.BlockSpec((1, tr, DE), lambda n, i: (n, i, 0))),
        compiler_params=_params(("arbitrary", "arbitrary"), 40),
        name="deinterleave",
    )(w_gu, perm)


def _ffn_kernel(be_ref, nu_ref, x_ref, wg_ref, bg_ref, wl_ref, bl_ref, wd_ref, bd_ref, o_ref):
    del be_ref
    b = pl.program_id(0)

    @pl.when(b >= nu_ref[0])
    def _():
        o_ref[...] = jnp.zeros(o_ref.shape, U32)

    @pl.when(b < nu_ref[0])
    def _():
        x = _unpack_pairs(x_ref[...])
        glu = jnp.dot(x, wg_ref[0], preferred_element_type=F32) + bg_ref[0]
        lin = jnp.dot(x, wl_ref[0], preferred_element_type=F32) + bl_ref[0]
        glu = jnp.minimum(glu, SWIGLU_LIMIT)
        lin = jnp.clip(lin, -SWIGLU_LIMIT, SWIGLU_LIMIT)
        act = glu * _sigmoid(SWIGLU_ALPHA * glu) * (lin + 1.0)
        y = jnp.dot(act.astype(BF16), wd_ref[0], preferred_element_type=F32) + bd_ref[0]
        o_ref[...] = _pack_pairs(y)


def _ffn(block_exp, n_used, xb, wg, bg, wl, bl, wd, bd, tm):
    m_pad, W = xb.shape
    _, D, DE = wg.shape
    n_blocks = m_pad // tm

    def rows(b, be, nu):
        return (jnp.minimum(b, nu[0] - 1), 0)

    def expert(b, be, nu):
        return (be[jnp.minimum(b, nu[0] - 1)], 0, 0)

    grid_spec = pltpu.PrefetchScalarGridSpec(
        num_scalar_prefetch=2,
        grid=(n_blocks,),
        in_specs=[
            pl.BlockSpec((tm, W), rows),
            pl.BlockSpec((1, D, DE), expert),
            pl.BlockSpec((1, 1, DE), expert),
            pl.BlockSpec((1, D, DE), expert),
            pl.BlockSpec((1, 1, DE), expert),
            pl.BlockSpec((1, DE, D), expert),
            pl.BlockSpec((1, 1, D), expert),
        ],
        out_specs=pl.BlockSpec((tm, W), lambda b, be, nu: (b, 0)),
    )
    return pl.pallas_call(
        _ffn_kernel,
        out_shape=jax.ShapeDtypeStruct((m_pad, W), U32),
        grid_spec=grid_spec,
        compiler_params=_params(("arbitrary",), 56),
        name="moe_ffn",
    )(block_exp, n_used, xb, wg, bg, wl, bl, wd, bd)


def _combine_kernel(dest_ref, gate_ref, h_ref, g2_ref, b2_ref, yb_ref, of_ref, ob_ref, buf, sem, *, alpha):
    tm, D = h_ref.shape

    def row(r, c):
        for j in range(TOP_K):
            d = dest_ref[r * TOP_K + j]
            pltpu.make_async_copy(yb_ref.at[pl.ds(d, 1), :], buf.at[j, pl.ds(r, 1), :], sem).start()
        return c

    lax.fori_loop(0, tm, row, 0, unroll=ROW_UNROLL)
    for j in range(TOP_K):
        pltpu.make_async_copy(yb_ref.at[pl.ds(0, tm), :], buf.at[j], sem).wait()

    gate = gate_ref[...]
    moe = None
    for j in range(TOP_K):
        term = gate[:, j:j + 1] * _unpack_pairs(buf[j]).astype(F32)
        moe = term if moe is None else moe + term
    y = _layer_norm(alpha * h_ref[...] + moe, g2_ref[...], b2_ref[...])
    of_ref[...] = y
    ob_ref[...] = y.astype(BF16)


def _combine(dest_flat, gate, h, g2, b2, yb, alpha, tm):
    R, D = h.shape
    W = yb.shape[1]
    row = pl.BlockSpec((tm, D), lambda i: (i, 0))
    vec = pl.BlockSpec((1, D), lambda i: (0, 0))
    return pl.pallas_call(
        functools.partial(_combine_kernel, alpha=alpha),
        out_shape=(jax.ShapeDtypeStruct((R, D), F32), jax.ShapeDtypeStruct((R, D), BF16)),
        grid=(R // tm,),
        in_specs=[
            pl.BlockSpec((tm * TOP_K,), lambda i: (i,), memory_space=pltpu.SMEM),
            pl.BlockSpec((tm, LANES), lambda i: (i, 0)),
            row, vec, vec,
            pl.BlockSpec(memory_space=pl.ANY),
        ],
        out_specs=(row, row),
        scratch_shapes=[pltpu.VMEM((TOP_K, tm, W), U32), pltpu.SemaphoreType.DMA],
        compiler_params=_params(("arbitrary",), 48),
        name="moe_combine",
    )(dest_flat, gate, h, g2.reshape(1, D), b2.reshape(1, D), yb)


def _lane_pad(v, offset):
    return jnp.zeros((1, LANES), F32).at[0, offset:offset + v.shape[0]].set(v.astype(F32))


def kernel(x, meta_tokens, ln_in_g, ln_in_b, w_in, rg_conv_w, rg_conv_b, rg_wx, rg_bx, rg_wa, rg_ba, rg_lambda, gdn_conv_w, gdn_a_log, gdn_dt_bias, gdn_norm_w, w_branch_a, w_branch_b, w_out, ln1_g, ln1_b, w_router, b_router, w_gate_up, b_gate_up, w_down, b_down, ln2_g, ln2_b):
    B, seq, D = x.shape
    depth = w_in.shape[0]
    E = w_router.shape[2]
    DE = w_down.shape[2]
    d_rnn = D // 2
    assert D == 2 * GDN_V_DIM and d_rnn == GDN_V_DIM and E <= LANES
    alpha = float((2 * depth) ** 0.25)

    T = N_META + seq
    pad = (-T) % CHUNK
    Tp = T + pad
    R = B * Tp

    tm_ln = _pick_tile(R, 512)
    tm_in = _pick_tile(R, 1040)
    tn_in = 1024
    tt_rg = _pick_tile(Tp, 416, SUBLANES)
    tt_gdn = _pick_tile(Tp, 320, CHUNK)
    tm_mg = _pick_tile(R, 256)
    tm_moe = _pick_tile(R, 256)
    n_blocks = -(-(R * TOP_K) // tm_moe) + E
    m_pad = n_blocks * tm_moe

    meta = jnp.broadcast_to(meta_tokens[None].astype(x.dtype), (B, N_META, D))
    h0 = jnp.concatenate([jnp.zeros((B, pad, D), x.dtype), meta, x], axis=1).reshape(R, D)
    h, hb = _ln_in(h0, ln_in_g, ln_in_b, tm_ln)

    wg_all, wl_all = _deinterleave(w_gate_up.reshape(depth * E, D, 2 * DE), _pick_tile(D, 512))
    wd_all = w_down.reshape(depth * E, DE, D).astype(BF16)
    bg_all = b_gate_up.reshape(depth * E, 1, 2 * DE)[:, :, 0::2]
    bl_all = b_gate_up.reshape(depth * E, 1, 2 * DE)[:, :, 1::2]
    bd_all = b_down.reshape(depth * E, 1, D)

    o_rx = 0
    o_q = 2 * d_rnn
    o_z = o_q + 2 * GDN_QK_DIM + GDN_V_DIM
    o_beta = o_z + GDN_V_DIM
    o_ga = o_beta + 2 * GDN_H_V
    o_gb = o_ga + D
    block_start = jnp.arange(n_blocks, dtype=I32) * tm_moe

    for l in range(depth):
        wl = w_in[l]
        w_main = jnp.concatenate(
            [wl[:, o_ga:o_ga + D], wl[:, o_gb:o_gb + D], wl[:, o_q:o_z], wl[:, o_rx:o_q], wl[:, o_z:o_beta]],
            axis=1).astype(BF16)
        w_small = jnp.zeros((D, LANES), F32).at[:, :2 * GDN_H_V].set(wl[:, o_beta:o_ga]).astype(BF16)
        proj, small = _inproj(hb, w_main, w_small, tm_in, tn_in)

        ya = _rglru(proj, rg_conv_w[l], rg_conv_b[l], rg_wx[l].astype(BF16), rg_bx[l], rg_wa[l].astype(BF16),
                    rg_ba[l], rg_lambda[l], B, Tp, pad, tt_rg, xcol=(2 * D + 2048) // d_rnn,
                    ycol=(2 * D + 2048) // d_rnn + 1)
        yb = _gdn(proj, small, gdn_conv_w[l], _lane_pad(gdn_a_log[l], GDN_H_V), _lane_pad(gdn_dt_bias[l], GDN_H_V),
                  gdn_norm_w[l].reshape(1, GDN_DV).astype(F32), B, Tp, pad, tt_gdn, qkvcol=2,
                  zcol=(2 * D + 2048 + 2 * d_rnn) // GDN_V_DIM)

        wr = jnp.zeros((D, LANES), F32).at[:, :E].set(w_router[l])
        br = jnp.full((1, LANES), -jnp.inf, F32).at[0, :E].set(b_router[l])
        hn, hp, idx, gate, counts = _merge(
            ya, yb, proj, h, w_branch_a[l].astype(BF16), w_branch_b[l].astype(BF16), w_out[l].astype(BF16),
            ln1_g[l], ln1_b[l], wr, br, alpha, tm_mg, gacol=0, gbcol=1)

        cnt = counts[0, :E].astype(I32)
        padded = (cnt + tm_moe - 1) // tm_moe * tm_moe
        pad_ends = jnp.cumsum(padded)
        starts = pad_ends - padded
        block_exp = jnp.minimum(jnp.sum((pad_ends[None, :] <= block_start[:, None]).astype(I32), axis=1), E - 1)
        onehot = (block_exp[:, None] == jnp.arange(E, dtype=I32)[None, :]).astype(I32)
        n_valid = jnp.clip(jnp.sum(onehot * (cnt + starts)[None, :], axis=1) - block_start, 0, tm_moe).astype(I32)
        n_used = (pad_ends[E - 1:E] // tm_moe).astype(I32)

        dest = _dest(idx, _lane_pad(starts, 0), tm_moe)
        dest_flat = dest[:, :TOP_K].reshape(R * TOP_K)

        xb = _dispatch(n_valid, dest_flat, hp, m_pad, tm_moe)
        ybk = _ffn(block_exp + l * E, n_used, xb, wg_all, bg_all, wl_all, bl_all, wd_all, bd_all, tm_moe)
        h, hb = _combine(dest_flat, gate, hn, ln2_g[l], ln2_b[l], ybk, alpha, tm_moe)

    return h.reshape(B, Tp, D)[:, pad + N_META:]
```

```python
import functools

import jax
import jax.numpy as jnp
from jax import lax
from jax.experimental import pallas as pl
from jax.experimental.pallas import tpu as pltpu

F32 = jnp.float32
BF16 = jnp.bfloat16
I32 = jnp.int32
U32 = jnp.uint32

N_META = 16
CONV_W = 4
CHUNK = 64
TOP_K = 4
RG_BLOCKS = 8
RG_C = 8.0
GDN_DK = 128
GDN_DV = 128
GDN_H_QK = 4
GDN_H_V = 8
GDN_QK_DIM = GDN_H_QK * GDN_DK
GDN_V_DIM = GDN_H_V * GDN_DV
SWIGLU_LIMIT = 7.0
SWIGLU_ALPHA = 1.702
LN_EPS = 1e-5
RMS_EPS = 1e-6

LANES = 128
SUBLANES = 8
MIB = 1024 * 1024
HIGHEST = lax.Precision.HIGHEST


def _pick_tile(n, target, mult=16):
    best = None
    for t in range(mult, min(n, target) + 1, mult):
        if n % t == 0:
            best = t
    assert best is not None, (n, target, mult)
    return best


def _params(sem, vmem_mib):
    return pltpu.CompilerParams(dimension_semantics=sem, vmem_limit_bytes=vmem_mib * MIB)


def _layer_norm(x, g, b):
    mu = jnp.mean(x, axis=-1, keepdims=True)
    xc = x - mu
    var = jnp.mean(xc * xc, axis=-1, keepdims=True)
    return xc * lax.rsqrt(var + LN_EPS) * g + b


def _sigmoid(x):
    return 1.0 / (1.0 + jnp.exp(-x))


def _softplus(x):
    return jnp.maximum(x, 0.0) + jnp.log1p(jnp.exp(-jnp.abs(x)))


def _silu(x):
    return x * _sigmoid(x)


def _gelu_tanh(x):
    return 0.5 * x * (1.0 + jnp.tanh(0.7978845608028654 * (x + 0.044715 * x * x * x)))


def _pack_pairs(x):
    n = x.shape[1] // 2
    xb = x.astype(BF16).astype(F32)
    lo = lax.bitcast_convert_type(xb[:, :n], U32) >> 16
    hi = lax.bitcast_convert_type(xb[:, n:], U32) & jnp.uint32(0xFFFF0000)
    return lo | hi


def _unpack_pairs(w):
    lo = lax.bitcast_convert_type(w << 16, F32)
    hi = lax.bitcast_convert_type(w & jnp.uint32(0xFFFF0000), F32)
    return jnp.concatenate([lo, hi], axis=1).astype(BF16)


def _ln_in_kernel(x_ref, g_ref, b_ref, of_ref, ob_ref):
    y = _layer_norm(x_ref[...], g_ref[...], b_ref[...])
    of_ref[...] = y
    ob_ref[...] = y.astype(BF16)


def _ln_in(x2d, g, b, tm):
    R, D = x2d.shape
    row = pl.BlockSpec((tm, D), lambda i: (i, 0))
    vec = pl.BlockSpec((1, D), lambda i: (0, 0))
    return pl.pallas_call(
        _ln_in_kernel,
        out_shape=(jax.ShapeDtypeStruct((R, D), F32), jax.ShapeDtypeStruct((R, D), BF16)),
        grid=(R // tm,),
        in_specs=[row, vec, vec],
        out_specs=(row, row),
        compiler_params=_params(("arbitrary",), 32),
        name="ln_in",
    )(x2d, g.reshape(1, D), b.reshape(1, D))


def _inproj_kernel(x_ref, w_ref, ws_ref, o_ref, os_ref):
    x = x_ref[...]
    o_ref[...] = jnp.dot(x, w_ref[...], preferred_element_type=F32)

    @pl.when(pl.program_id(1) == 0)
    def _():
        os_ref[...] = jnp.dot(x, ws_ref[...], preferred_element_type=F32)


def _inproj(hb, w_main, w_small, tm, tn):
    R, D = hb.shape
    N = w_main.shape[1]
    return pl.pallas_call(
        _inproj_kernel,
        out_shape=(jax.ShapeDtypeStruct((R, N), F32), jax.ShapeDtypeStruct((R, LANES), F32)),
        grid=(R // tm, N // tn),
        in_specs=[
            pl.BlockSpec((tm, D), lambda i, j: (i, 0)),
            pl.BlockSpec((D, tn), lambda i, j: (0, j)),
            pl.BlockSpec((D, LANES), lambda i, j: (0, 0)),
        ],
        out_specs=(
            pl.BlockSpec((tm, tn), lambda i, j: (i, j)),
            pl.BlockSpec((tm, LANES), lambda i, j: (i, 0)),
        ),
        compiler_params=_params(("arbitrary", "arbitrary"), 48),
        name="inproj",
    )(hb, w_main, w_small)


def _causal_conv(x, xbuf, cw_ref, tt):
    xbuf[SUBLANES:SUBLANES + tt, :] = x
    y = cw_ref[CONV_W - 1:CONV_W, :] * x
    for j in range(CONV_W - 1):
        off = SUBLANES - (CONV_W - 1) + j
        y = y + cw_ref[j:j + 1, :] * xbuf[off:off + tt, :]
    xbuf[0:SUBLANES, :] = xbuf[tt:tt + SUBLANES, :]
    return y


def _rglru_kernel(x_ref, y_ref, cw_ref, cb_ref, wx_ref, bx_ref, wa_ref, ba_ref, lam_ref,
                  o_ref, xbuf, a_s, u_s, hc, *, pad):
    t = pl.program_id(1)
    tt, C = x_ref.shape
    bs = C // RG_BLOCKS

    @pl.when(t == 0)
    def _():
        xbuf[0:SUBLANES, :] = jnp.zeros((SUBLANES, C), F32)
        hc[...] = jnp.zeros((SUBLANES, C), F32)

    row = lax.broadcasted_iota(I32, (tt, 1), 0)
    valid = (row + t * tt) >= pad
    x = jnp.where(valid, x_ref[...], 0.0)
    xc = _causal_conv(x, xbuf, cw_ref, tt) + cb_ref[...]

    xcb = xc.astype(BF16)
    gi = jnp.concatenate(
        [jnp.dot(xcb[:, h * bs:(h + 1) * bs], wx_ref[h], preferred_element_type=F32) for h in range(RG_BLOCKS)],
        axis=1)
    gr = jnp.concatenate(
        [jnp.dot(xcb[:, h * bs:(h + 1) * bs], wa_ref[h], preferred_element_type=F32) for h in range(RG_BLOCKS)],
        axis=1)
    gi = _sigmoid(gi + bx_ref[...])
    gr = _sigmoid(gr + ba_ref[...])
    log_a = (-RG_C) * gr * _softplus(-lam_ref[...])
    a = jnp.exp(log_a)
    u = jnp.sqrt(-jnp.tanh(log_a) * (a * a + 1.0)) * gi * xc
    u = jnp.where(valid, u, 0.0)

    rowmod = row % SUBLANES
    for d in (1, 2, 4):
        m = rowmod >= d
        a_sh = pltpu.roll(a, d, 0)
        u_sh = pltpu.roll(u, d, 0)
        u = jnp.where(m, a * u_sh + u, u)
        a = jnp.where(m, a * a_sh, a)
    a_s[...] = a
    u_s[...] = u

    def group(g, hprev):
        r0 = pl.multiple_of(g * SUBLANES, SUBLANES)
        h = a_s[pl.ds(r0, SUBLANES), :] * hprev + u_s[pl.ds(r0, SUBLANES), :]
        u_s[pl.ds(r0, SUBLANES), :] = h
        return jnp.broadcast_to(h[SUBLANES - 1:SUBLANES, :], (SUBLANES, C))

    hc[...] = lax.fori_loop(0, tt // SUBLANES, group, hc[...])
    o_ref[...] = (u_s[...] * _gelu_tanh(y_ref[...])).astype(BF16)


def _rglru(proj, cw, cb, wx, bx, wa, ba, lam, B, Tp, pad, tt, xcol, ycol):
    R = proj.shape[0]
    C = cw.shape[1]
    nT = Tp // tt
    vec = pl.BlockSpec((1, C), lambda b, t: (0, 0))
    wspec = pl.BlockSpec(wx.shape, lambda b, t: (0, 0, 0))
    return pl.pallas_call(
        functools.partial(_rglru_kernel, pad=pad),
        out_shape=jax.ShapeDtypeStruct((R, C), BF16),
        grid=(B, nT),
        in_specs=[
            pl.BlockSpec((tt, C), lambda b, t: (b * nT + t, xcol)),
            pl.BlockSpec((tt, C), lambda b, t: (b * nT + t, ycol)),
            pl.BlockSpec((CONV_W, C), lambda b, t: (0, 0)),
            vec, wspec, vec, wspec, vec, vec,
        ],
        out_specs=pl.BlockSpec((tt, C), lambda b, t: (b * nT + t, 0)),
        scratch_shapes=[
            pltpu.VMEM((tt + SUBLANES, C), F32),
            pltpu.VMEM((tt, C), F32),
            pltpu.VMEM((tt, C), F32),
            pltpu.VMEM((SUBLANES, C), F32),
        ],
        compiler_params=_params(("arbitrary", "arbitrary"), 48),
        name="rglru",
    )(proj, proj, cw, cb.reshape(1, C), wx, bx.reshape(1, C), wa, ba.reshape(1, C), lam.reshape(1, C))


def _bdot(a, b):
    return jnp.dot(a.astype(BF16), b.astype(BF16), preferred_element_type=F32)


def _bdot_nt(a, b):
    return lax.dot_general(a.astype(BF16), b.astype(BF16), (((1,), (1,)), ((), ())), preferred_element_type=F32)


def _bdot_tn(a, b):
    return lax.dot_general(a.astype(BF16), b.astype(BF16), (((0,), (0,)), ((), ())), preferred_element_type=F32)


def _unit_lower_inverses(Ls, masks):
    eye = masks[0]
    dinvs = [eye - L * masks[1] for L in Ls]
    for m in masks[2:]:
        ts = [_bdot(d, L * m) for d, L in zip(dinvs, Ls)]
        dinvs = [d - _bdot(t, d) for d, t in zip(dinvs, ts)]
    return dinvs


def _gdn_kernel(qkv_ref, z_ref, sm_ref, cw_ref, alog_ref, dtb_ref, nw_ref,
                o_ref, xbuf, q_s, k_s, v_s, b_s, g_s, S_ref, val_s, kq_s, at_s, kd_s, eg_s, *, pad):
    t = pl.program_id(1)
    tt, CQ = qkv_ref.shape
    ncht = tt // CHUNK
    rep = GDN_H_V // GDN_H_QK

    @pl.when(t == 0)
    def _():
        xbuf[0:SUBLANES, :] = jnp.zeros((SUBLANES, CQ), F32)
        S_ref[...] = jnp.zeros(S_ref.shape, F32)

    row = lax.broadcasted_iota(I32, (tt, 1), 0)
    valid = (row + t * tt) >= pad
    x = jnp.where(valid, qkv_ref[...], 0.0)
    xc = _silu(_causal_conv(x, xbuf, cw_ref, tt))

    for hq in range(GDN_H_QK):
        qh = xc[:, hq * GDN_DK:(hq + 1) * GDN_DK]
        kh = xc[:, GDN_QK_DIM + hq * GDN_DK:GDN_QK_DIM + (hq + 1) * GDN_DK]
        qn = qh * lax.rsqrt(jnp.sum(qh * qh, axis=-1, keepdims=True) + RMS_EPS) * (GDN_DK ** -0.5)
        kn = kh * lax.rsqrt(jnp.sum(kh * kh, axis=-1, keepdims=True) + RMS_EPS)
        q_s[:, hq * GDN_DK:(hq + 1) * GDN_DK] = qn
        k_s[:, hq * GDN_DK:(hq + 1) * GDN_DK] = kn
    v_s[...] = xc[:, 2 * GDN_QK_DIM:]

    sm = sm_ref[...]
    b_s[...] = _sigmoid(sm)
    g = -jnp.exp(alog_ref[...]) * _softplus(sm + dtb_ref[...])
    g_s[...] = jnp.where(valid, g, 0.0)

    ri = lax.broadcasted_iota(I32, (CHUNK, CHUNK), 0)
    ci = lax.broadcasted_iota(I32, (CHUNK, CHUNK), 1)
    causal = ri >= ci
    strict = ri > ci
    tril = causal.astype(F32)
    masks = [(ri == ci).astype(F32)]
    s = 1
    while s < CHUNK:
        masks.append(((ri // (2 * s) == ci // (2 * s)) & (ri % (2 * s) >= s) & (ci % (2 * s) < s)).astype(F32))
        s *= 2
    nw = nw_ref[...]

    heads = range(GDN_H_V)
    chunks = range(ncht)
    inst = [(c, h) for c in chunks for h in heads]

    rows = [slice(c * CHUNK, (c + 1) * CHUNK) for c in chunks]
    G = [jnp.dot(tril, g_s[rows[c], :], precision=HIGHEST, preferred_element_type=F32) for c in chunks]
    GT = [G[c].T for c in chunks]
    bet = [b_s[rows[c], :] for c in chunks]
    qs = [[q_s[rows[c], hq * GDN_DK:(hq + 1) * GDN_DK] for hq in range(GDN_H_QK)] for c in chunks]
    ks_ = [[k_s[rows[c], hq * GDN_DK:(hq + 1) * GDN_DK] for hq in range(GDN_H_QK)] for c in chunks]
    kk = [[_bdot_nt(k, k) for k in ks_[c]] for c in chunks]
    qk = [[_bdot_nt(q, k) for q, k in zip(qs[c], ks_[c])] for c in chunks]
    bcol = {(c, h): bet[c][:, h:h + 1] for c, h in inst}
    Gc = {(c, h): G[c][:, GDN_H_V + h:GDN_H_V + h + 1] for c, h in inst}
    decay = {(c, h): jnp.where(causal, jnp.exp(jnp.minimum(
        Gc[c, h] - GT[c][GDN_H_V + h:GDN_H_V + h + 1, :], 0.0)), 0.0) for c, h in inst}
    eG = {i: jnp.exp(Gc[i]) for i in inst}
    g_last = {i: Gc[i][CHUNK - 1:CHUNK, :] for i in inst}
    Ls = [jnp.where(strict, kk[c][h // rep] * bcol[c, h] * decay[c, h], 0.0) for c, h in inst]
    Ts = dict(zip(inst, _unit_lower_inverses(Ls, masks)))
    for c, h in inst:
        k = ks_[c][h // rep]
        sol = _bdot(Ts[c, h], jnp.concatenate(
            [v_s[rows[c], h * GDN_DV:(h + 1) * GDN_DV] * bcol[c, h], k * (bcol[c, h] * eG[c, h])], axis=1))
        val_s[c, h] = sol[:, :GDN_DV]
        kq_s[c, h] = jnp.concatenate([sol[:, GDN_DV:], qs[c][h // rep] * eG[c, h]], axis=0).astype(BF16)
        at_s[c, h] = (qk[c][h // rep] * decay[c, h]).astype(BF16)
        kd_s[c, h] = (k * jnp.exp(g_last[c, h] - Gc[c, h])).astype(BF16)
        eg_s[c, h] = jnp.broadcast_to(jnp.exp(g_last[c, h]), (SUBLANES, GDN_DV))

    for c in chunks:
        zc = z_ref[rows[c], :]
        Sold = [S_ref[h] for h in heads]
        PS = [_bdot(kq_s[c, h], Sold[h]) for h in heads]
        v_new = [val_s[c, h] - PS[h][:CHUNK] for h in heads]
        o = [PS[h][CHUNK:] + _bdot(at_s[c, h], v_new[h]) for h in heads]
        for h in heads:
            S_ref[h] = Sold[h] * eg_s[c, h][0:1, :] + _bdot_tn(kd_s[c, h], v_new[h])
        for h in heads:
            oh = o[h] * lax.rsqrt(jnp.mean(o[h] * o[h], axis=-1, keepdims=True) + RMS_EPS) * nw
            oh = oh * _silu(zc[:, h * GDN_DV:(h + 1) * GDN_DV])
            o_ref[rows[c], h * GDN_DV:(h + 1) * GDN_DV] = oh.astype(BF16)


def _gdn(proj, small, cw, alog_pad, dtb_pad, nw, B, Tp, pad, tt, qkvcol, zcol):
    R = proj.shape[0]
    CQ = cw.shape[1]
    nT = Tp // tt
    lane_vec = pl.BlockSpec((1, LANES), lambda b, t: (0, 0))
    return pl.pallas_call(
        functools.partial(_gdn_kernel, pad=pad),
        out_shape=jax.ShapeDtypeStruct((R, GDN_V_DIM), BF16),
        grid=(B, nT),
        in_specs=[
            pl.BlockSpec((tt, CQ), lambda b, t: (b * nT + t, qkvcol)),
            pl.BlockSpec((tt, GDN_V_DIM), lambda b, t: (b * nT + t, zcol)),
            pl.BlockSpec((tt, LANES), lambda b, t: (b * nT + t, 0)),
            pl.BlockSpec((CONV_W, CQ), lambda b, t: (0, 0)),
            lane_vec, lane_vec, lane_vec,
        ],
        out_specs=pl.BlockSpec((tt, GDN_V_DIM), lambda b, t: (b * nT + t, 0)),
        scratch_shapes=[
            pltpu.VMEM((tt + SUBLANES, CQ), F32),
            pltpu.VMEM((tt, GDN_QK_DIM), F32),
            pltpu.VMEM((tt, GDN_QK_DIM), F32),
            pltpu.VMEM((tt, GDN_V_DIM), F32),
            pltpu.VMEM((tt, LANES), F32),
            pltpu.VMEM((tt, LANES), F32),
            pltpu.VMEM((GDN_H_V, GDN_DK, GDN_DV), F32),
            pltpu.VMEM((tt // CHUNK, GDN_H_V, CHUNK, GDN_DV), F32),
            pltpu.VMEM((tt // CHUNK, GDN_H_V, 2 * CHUNK, GDN_DK), BF16),
            pltpu.VMEM((tt // CHUNK, GDN_H_V, CHUNK, CHUNK), BF16),
            pltpu.VMEM((tt // CHUNK, GDN_H_V, CHUNK, GDN_DK), BF16),
            pltpu.VMEM((tt // CHUNK, GDN_H_V, SUBLANES, GDN_DV), F32),
        ],
        compiler_params=_params(("arbitrary", "arbitrary"), 48),
        name="gdn",
    )(proj, proj, small, cw, alog_pad, dtb_pad, nw)


def _merge_kernel(ya_ref, yb_ref, ga_ref, gb_ref, h_ref, wa_ref, wb_ref, wo_ref, g1_ref, b1_ref,
                  wrh_ref, wrl_ref, br_ref, hf_ref, hp_ref, idx_ref, gate_ref, cnt_ref, *, alpha):
    i = pl.program_id(0)
    tm, D = h_ref.shape
    ma = jnp.dot(ya_ref[...], wa_ref[...], preferred_element_type=F32)
    mb = jnp.dot(yb_ref[...], wb_ref[...], preferred_element_type=F32)
    mixed = _sigmoid(ga_ref[...]) * ma + _sigmoid(gb_ref[...]) * mb
    out = jnp.dot(mixed.astype(BF16), wo_ref[...], preferred_element_type=F32)
    hn = _layer_norm(alpha * h_ref[...] + out, g1_ref[...], b1_ref[...])
    hf_ref[...] = hn
    hp_ref[...] = _pack_pairs(hn)

    hn_hi = hn.astype(BF16)
    hn_lo = (hn - hn_hi.astype(F32)).astype(BF16)
    logits = (jnp.dot(hn_hi, wrh_ref[...], preferred_element_type=F32)
              + jnp.dot(hn_lo, wrh_ref[...], preferred_element_type=F32)
              + jnp.dot(hn_hi, wrl_ref[...], preferred_element_type=F32)) + br_ref[...]
    lane = lax.broadcasted_iota(I32, (tm, LANES), 1)
    vals = logits
    tops, sels = [], []
    idx_out = jnp.zeros((tm, LANES), I32)
    for j in range(TOP_K):
        m = jnp.max(vals, axis=-1, keepdims=True)
        idx = jnp.min(jnp.where(vals == m, lane, LANES), axis=-1, keepdims=True)
        sel = lane == idx
        tops.append(m)
        sels.append(sel)
        idx_out = jnp.where(lane == j, idx, idx_out)
        vals = jnp.where(sel, -jnp.inf, vals)
    es = [jnp.exp(m - tops[0]) for m in tops]
    inv = 1.0 / (es[0] + es[1] + es[2] + es[3])
    gate_out = jnp.zeros((tm, LANES), F32)
    onehot = jnp.zeros((tm, LANES), F32)
    for j in range(TOP_K):
        gate_out = jnp.where(lane == j, es[j] * inv, gate_out)
        onehot = onehot + sels[j].astype(F32)
    idx_ref[...] = idx_out
    gate_ref[...] = gate_out

    @pl.when(i == 0)
    def _():
        cnt_ref[...] = jnp.zeros(cnt_ref.shape, F32)

    cnt_ref[0:1, :] = cnt_ref[0:1, :] + jnp.sum(onehot, axis=0, keepdims=True)


def _merge(ya, yb, proj, h, wa, wb, wo, g1, b1, wr, br, alpha, tm, gacol, gbcol):
    R, D = h.shape
    C = ya.shape[1]
    const2 = lambda i: (0, 0)
    single = pl.Buffered(1)
    wr_hi = wr.astype(BF16)
    wr_lo = (wr - wr_hi.astype(F32)).astype(BF16)
    return pl.pallas_call(
        functools.partial(_merge_kernel, alpha=alpha),
        out_shape=(
            jax.ShapeDtypeStruct((R, D), F32),
            jax.ShapeDtypeStruct((R, D // 2), U32),
            jax.ShapeDtypeStruct((R, LANES), I32),
            jax.ShapeDtypeStruct((R, LANES), F32),
            jax.ShapeDtypeStruct((SUBLANES, LANES), F32),
        ),
        grid=(R // tm,),
        in_specs=[
            pl.BlockSpec((tm, C), lambda i: (i, 0)),
            pl.BlockSpec((tm, C), lambda i: (i, 0)),
            pl.BlockSpec((tm, D), lambda i: (i, gacol)),
            pl.BlockSpec((tm, D), lambda i: (i, gbcol)),
            pl.BlockSpec((tm, D), lambda i: (i, 0)),
            pl.BlockSpec((C, D), const2, pipeline_mode=single),
            pl.BlockSpec((C, D), const2, pipeline_mode=single),
            pl.BlockSpec((D, D), const2, pipeline_mode=single),
            pl.BlockSpec((1, D), const2),
            pl.BlockSpec((1, D), const2),
            pl.BlockSpec((D, LANES), const2, pipeline_mode=single),
            pl.BlockSpec((D, LANES), const2, pipeline_mode=single),
            pl.BlockSpec((1, LANES), const2),
        ],
        out_specs=(
            pl.BlockSpec((tm, D), lambda i: (i, 0)),
            pl.BlockSpec((tm, D // 2), lambda i: (i, 0)),
            pl.BlockSpec((tm, LANES), lambda i: (i, 0)),
            pl.BlockSpec((tm, LANES), lambda i: (i, 0)),
            pl.BlockSpec((SUBLANES, LANES), const2),
        ),
        compiler_params=_params(("arbitrary",), 56),
        name="merge",
    )(ya, yb, proj, proj, h, wa, wb, wo, g1.reshape(1, D), b1.reshape(1, D), wr_hi, wr_lo, br)


def _dest_kernel(idx_ref, start_ref, dest_ref, carry):
    i = pl.program_id(0)
    tr = idx_ref.shape[0]

    @pl.when(i == 0)
    def _():
        carry[...] = jnp.zeros(carry.shape, F32)

    idx = idx_ref[...]
    lane = lax.broadcasted_iota(I32, (tr, LANES), 1)
    ohs = [(lane == idx[:, j:j + 1]).astype(F32) for j in range(TOP_K)]
    oh = ohs[0] + ohs[1] + ohs[2] + ohs[3]
    ri = lax.broadcasted_iota(I32, (tr, tr), 0)
    ci = lax.broadcasted_iota(I32, (tr, tr), 1)
    stril = (ri > ci).astype(BF16)
    before = jnp.dot(stril, oh.astype(BF16), preferred_element_type=F32) + carry[0:1, :] + start_ref[...]
    dest = jnp.zeros((tr, LANES), I32)
    for j in range(TOP_K):
        pos = jnp.sum(ohs[j] * before, axis=-1, keepdims=True)
        dest = jnp.where(lane == j, pos.astype(I32), dest)
    dest_ref[...] = dest
    carry[0:1, :] = carry[0:1, :] + jnp.sum(oh, axis=0, keepdims=True)


def _dest(idx, starts, tr):
    R = idx.shape[0]
    return pl.pallas_call(
        _dest_kernel,
        out_shape=jax.ShapeDtypeStruct((R, LANES), I32),
        grid=(R // tr,),
        in_specs=[pl.BlockSpec((tr, LANES), lambda i: (i, 0)), pl.BlockSpec((1, LANES), lambda i: (0, 0))],
        out_specs=pl.BlockSpec((tr, LANES), lambda i: (i, 0)),
        scratch_shapes=[pltpu.VMEM((SUBLANES, LANES), F32)],
        compiler_params=_params(("arbitrary",), 32),
        name="moe_dest",
    )(idx, starts)


ROW_UNROLL = 4


def _dispatch_kernel(nv_ref, dest_ref, x_ref, xb_ref, zbuf, sem, zsem):
    tm = x_ref.shape[0]
    n_blocks = xb_ref.shape[0] // tm

    @pl.when(pl.program_id(0) == 0)
    def _():
        zbuf[...] = jnp.zeros(zbuf.shape, U32)

        def zero_copy(b):
            return pltpu.make_async_copy(zbuf, xb_ref.at[pl.ds(pl.multiple_of(b * tm, tm), tm), :], zsem)

        def start(b, c):
            @pl.when(nv_ref[b] < tm)
            def _():
                zero_copy(b).start()
            return c

        def wait(b, c):
            @pl.when(nv_ref[b] < tm)
            def _():
                zero_copy(b).wait()
            return c

        lax.fori_loop(0, n_blocks, start, 0)
        lax.fori_loop(0, n_blocks, wait, 0)

    def row(r, c):
        for j in range(TOP_K):
            d = dest_ref[r * TOP_K + j]
            pltpu.make_async_copy(x_ref.at[pl.ds(r, 1), :], xb_ref.at[pl.ds(d, 1), :], sem).start(priority=j % 2)
        return c

    lax.fori_loop(0, tm, row, 0, unroll=ROW_UNROLL)
    for j in range(TOP_K):
        pltpu.make_async_copy(x_ref, xb_ref.at[pl.ds(0, tm), :], sem).wait()


def _dispatch(n_valid, dest_flat, hp, m_pad, tm):
    R, W = hp.shape
    grid_spec = pltpu.PrefetchScalarGridSpec(
        num_scalar_prefetch=1,
        grid=(R // tm,),
        in_specs=[
            pl.BlockSpec((tm * TOP_K,), lambda i, nv: (i,), memory_space=pltpu.SMEM),
            pl.BlockSpec((tm, W), lambda i, nv: (i, 0)),
        ],
        out_specs=pl.BlockSpec(memory_space=pl.ANY),
        scratch_shapes=[pltpu.VMEM((tm, W), U32), pltpu.SemaphoreType.DMA, pltpu.SemaphoreType.DMA],
    )
    return pl.pallas_call(
        _dispatch_kernel,
        out_shape=jax.ShapeDtypeStruct((m_pad, W), U32),
        grid_spec=grid_spec,
        compiler_params=_params(("arbitrary",), 32),
        name="moe_dispatch",
    )(n_valid, dest_flat, hp)


def _deint_kernel(w_ref, p_ref, g_ref, l_ref):
    w = w_ref[0].astype(BF16)
    perm = p_ref[...]
    width = perm.shape[0]
    half = width // 2
    for g in range(w.shape[1] // width):
        t = jnp.dot(w[:, g * width:(g + 1) * width], perm, preferred_element_type=F32)
        g_ref[0, :, g * half:(g + 1) * half] = t[:, :half].astype(BF16)
        l_ref[0, :, g * half:(g + 1) * half] = t[:, half:].astype(BF16)


def _deinterleave(w_gu, tr):
    N, D, DE2 = w_gu.shape
    DE = DE2 // 2
    width = 2 * LANES
    src = lax.broadcasted_iota(I32, (width, width), 0)
    dst = lax.broadcasted_iota(I32, (width, width), 1)
    perm = (src == jnp.where(dst < LANES, 2 * dst, 2 * (dst - LANES) + 1)).astype(BF16)
    out = jax.ShapeDtypeStruct((N, D, DE), BF16)
    return pl.pallas_call(
        _deint_kernel,
        out_shape=(out, out),
        grid=(N, D // tr),
        in_specs=[
            pl.BlockSpec((1, tr, DE2), lambda n, i: (n, i, 0)),
            pl.BlockSpec((width, width), lambda n, i: (0, 0)),
        ],
        out_specs=(pl.BlockSpec((1, tr, DE), lambda n, i: (n, i, 0)),
                   pl.BlockSpec((1, tr, DE), lambda n, i: (n, i, 0))),
        compiler_params=_params(("arbitrary", "arbitrary"), 40),
        name="deinterleave",
    )(w_gu, perm)


def _ffn_kernel(be_ref, nv_ref, nu_ref, x_ref, wg_ref, bg_ref, wl_ref, bl_ref, wd_ref, bd_ref, o_ref):
    del be_ref
    b = pl.program_id(0)
    tm = x_ref.shape[0]
    half = tm // 2
    used = b < nu_ref[0]
    n_valid = nv_ref[b]

    def ffn_rows(rows):
        x = _unpack_pairs(x_ref[rows, :])
        glu = jnp.dot(x, wg_ref[0], preferred_element_type=F32) + bg_ref[0]
        lin = jnp.dot(x, wl_ref[0], preferred_element_type=F32) + bl_ref[0]
        glu = jnp.minimum(glu, SWIGLU_LIMIT)
        lin = jnp.clip(lin, -SWIGLU_LIMIT, SWIGLU_LIMIT)
        act = glu * _sigmoid(SWIGLU_ALPHA * glu) * (lin + 1.0)
        y = jnp.dot(act.astype(BF16), wd_ref[0], preferred_element_type=F32) + bd_ref[0]
        o_ref[rows, :] = _pack_pairs(y)

    @pl.when(jnp.logical_not(used))
    def _():
        o_ref[...] = jnp.zeros(o_ref.shape, U32)

    @pl.when(jnp.logical_and(used, n_valid > half))
    def _():
        ffn_rows(slice(0, tm))

    @pl.when(jnp.logical_and(used, n_valid <= half))
    def _():
        ffn_rows(slice(0, half))
        o_ref[half:, :] = jnp.zeros((tm - half, o_ref.shape[1]), U32)


def _ffn(block_exp, n_valid, n_used, xb, wg, bg, wl, bl, wd, bd, tm):
    m_pad, W = xb.shape
    _, D, DE = wg.shape
    n_blocks = m_pad // tm

    def rows(b, be, nv, nu):
        return (jnp.minimum(b, nu[0] - 1), 0)

    def expert(b, be, nv, nu):
        return (be[jnp.minimum(b, nu[0] - 1)], 0, 0)

    grid_spec = pltpu.PrefetchScalarGridSpec(
        num_scalar_prefetch=3,
        grid=(n_blocks,),
        in_specs=[
            pl.BlockSpec((tm, W), rows),
            pl.BlockSpec((1, D, DE), expert),
            pl.BlockSpec((1, 1, DE), expert),
            pl.BlockSpec((1, D, DE), expert),
            pl.BlockSpec((1, 1, DE), expert),
            pl.BlockSpec((1, DE, D), expert),
            pl.BlockSpec((1, 1, D), expert),
        ],
        out_specs=pl.BlockSpec((tm, W), lambda b, be, nv, nu: (b, 0)),
    )
    return pl.pallas_call(
        _ffn_kernel,
        out_shape=jax.ShapeDtypeStruct((m_pad, W), U32),
        grid_spec=grid_spec,
        compiler_params=_params(("arbitrary",), 56),
        name="moe_ffn",
    )(block_exp, n_valid, n_used, xb, wg, bg, wl, bl, wd, bd)


def _combine_kernel(dest_ref, gate_ref, h_ref, g2_ref, b2_ref, yb_ref, of_ref, ob_ref, buf, sem, *, alpha):
    tm, D = h_ref.shape

    def row(r, c):
        for j in range(TOP_K):
            d = dest_ref[r * TOP_K + j]
            pltpu.make_async_copy(yb_ref.at[pl.ds(d, 1), :], buf.at[j, pl.ds(r, 1), :], sem).start(priority=j % 2)
        return c

    lax.fori_loop(0, tm, row, 0, unroll=ROW_UNROLL)
    for j in range(TOP_K):
        pltpu.make_async_copy(yb_ref.at[pl.ds(0, tm), :], buf.at[j], sem).wait()

    gate = gate_ref[...]
    moe = None
    for j in range(TOP_K):
        term = gate[:, j:j + 1] * _unpack_pairs(buf[j]).astype(F32)
        moe = term if moe is None else moe + term
    y = _layer_norm(alpha * h_ref[...] + moe, g2_ref[...], b2_ref[...])
    of_ref[...] = y
    ob_ref[...] = y.astype(BF16)


def _combine(dest_flat, gate, h, g2, b2, yb, alpha, tm):
    R, D = h.shape
    W = yb.shape[1]
    row = pl.BlockSpec((tm, D), lambda i: (i, 0))
    vec = pl.BlockSpec((1, D), lambda i: (0, 0))
    return pl.pallas_call(
        functools.partial(_combine_kernel, alpha=alpha),
        out_shape=(jax.ShapeDtypeStruct((R, D), F32), jax.ShapeDtypeStruct((R, D), BF16)),
        grid=(R // tm,),
        in_specs=[
            pl.BlockSpec((tm * TOP_K,), lambda i: (i,), memory_space=pltpu.SMEM),
            pl.BlockSpec((tm, LANES), lambda i: (i, 0)),
            row, vec, vec,
            pl.BlockSpec(memory_space=pl.ANY),
        ],
        out_specs=(row, row),
        scratch_shapes=[pltpu.VMEM((TOP_K, tm, W), U32), pltpu.SemaphoreType.DMA],
        compiler_params=_params(("arbitrary",), 48),
        name="moe_combine",
    )(dest_flat, gate, h, g2.reshape(1, D), b2.reshape(1, D), yb)


def _lane_pad(v, offset):
    return jnp.zeros((1, LANES), F32).at[0, offset:offset + v.shape[0]].set(v.astype(F32))


def kernel(x, meta_tokens, ln_in_g, ln_in_b, w_in, rg_conv_w, rg_conv_b, rg_wx, rg_bx, rg_wa, rg_ba, rg_lambda, gdn_conv_w, gdn_a_log, gdn_dt_bias, gdn_norm_w, w_branch_a, w_branch_b, w_out, ln1_g, ln1_b, w_router, b_router, w_gate_up, b_gate_up, w_down, b_down, ln2_g, ln2_b):
    B, seq, D = x.shape
    depth = w_in.shape[0]
    E = w_router.shape[2]
    DE = w_down.shape[2]
    d_rnn = D // 2
    assert D == 2 * GDN_V_DIM and d_rnn == GDN_V_DIM and E <= LANES
    alpha = float((2 * depth) ** 0.25)

    T = N_META + seq
    pad = (-T) % CHUNK
    Tp = T + pad
    R = B * Tp

    tm_ln = _pick_tile(R, 512)
    tm_in = _pick_tile(R, 1040)
    tn_in = 1024
    tt_rg = _pick_tile(Tp, 416, SUBLANES)
    tt_gdn = _pick_tile(Tp, 320, CHUNK)
    tm_mg = _pick_tile(R, 256)
    tm_moe = _pick_tile(R, 256)
    n_blocks = -(-(R * TOP_K) // tm_moe) + E
    m_pad = n_blocks * tm_moe

    meta = jnp.broadcast_to(meta_tokens[None].astype(x.dtype), (B, N_META, D))
    h0 = jnp.concatenate([jnp.zeros((B, pad, D), x.dtype), meta, x], axis=1).reshape(R, D)
    h, hb = _ln_in(h0, ln_in_g, ln_in_b, tm_ln)

    wg_all, wl_all = _deinterleave(w_gate_up.reshape(depth * E, D, 2 * DE), _pick_tile(D, 512))
    wd_all = w_down.reshape(depth * E, DE, D).astype(BF16)
    bg_all = b_gate_up.reshape(depth * E, 1, 2 * DE)[:, :, 0::2]
    bl_all = b_gate_up.reshape(depth * E, 1, 2 * DE)[:, :, 1::2]
    bd_all = b_down.reshape(depth * E, 1, D)

    o_rx = 0
    o_q = 2 * d_rnn
    o_z = o_q + 2 * GDN_QK_DIM + GDN_V_DIM
    o_beta = o_z + GDN_V_DIM
    o_ga = o_beta + 2 * GDN_H_V
    o_gb = o_ga + D
    block_start = jnp.arange(n_blocks, dtype=I32) * tm_moe

    for l in range(depth):
        wl = w_in[l]
        w_main = jnp.concatenate(
            [wl[:, o_ga:o_ga + D], wl[:, o_gb:o_gb + D], wl[:, o_q:o_z], wl[:, o_rx:o_q], wl[:, o_z:o_beta]],
            axis=1).astype(BF16)
        w_small = jnp.zeros((D, LANES), F32).at[:, :2 * GDN_H_V].set(wl[:, o_beta:o_ga]).astype(BF16)
        proj, small = _inproj(hb, w_main, w_small, tm_in, tn_in)

        ya = _rglru(proj, rg_conv_w[l], rg_conv_b[l], rg_wx[l].astype(BF16), rg_bx[l], rg_wa[l].astype(BF16),
                    rg_ba[l], rg_lambda[l], B, Tp, pad, tt_rg, xcol=(2 * D + 2048) // d_rnn,
                    ycol=(2 * D + 2048) // d_rnn + 1)
        yb = _gdn(proj, small, gdn_conv_w[l], _lane_pad(gdn_a_log[l], GDN_H_V), _lane_pad(gdn_dt_bias[l], GDN_H_V),
                  gdn_norm_w[l].reshape(1, GDN_DV).astype(F32), B, Tp, pad, tt_gdn, qkvcol=2,
                  zcol=(2 * D + 2048 + 2 * d_rnn) // GDN_V_DIM)

        wr = jnp.zeros((D, LANES), F32).at[:, :E].set(w_router[l])
        br = jnp.full((1, LANES), -jnp.inf, F32).at[0, :E].set(b_router[l])
        hn, hp, idx, gate, counts = _merge(
            ya, yb, proj, h, w_branch_a[l].astype(BF16), w_branch_b[l].astype(BF16), w_out[l].astype(BF16),
            ln1_g[l], ln1_b[l], wr, br, alpha, tm_mg, gacol=0, gbcol=1)

        cnt = counts[0, :E].astype(I32)
        padded = (cnt + tm_moe - 1) // tm_moe * tm_moe
        pad_ends = jnp.cumsum(padded)
        starts = pad_ends - padded
        block_exp = jnp.minimum(jnp.sum((pad_ends[None, :] <= block_start[:, None]).astype(I32), axis=1), E - 1)
        onehot = (block_exp[:, None] == jnp.arange(E, dtype=I32)[None, :]).astype(I32)
        n_valid = jnp.clip(jnp.sum(onehot * (cnt + starts)[None, :], axis=1) - block_start, 0, tm_moe).astype(I32)
        n_used = (pad_ends[E - 1:E] // tm_moe).astype(I32)

        dest = _dest(idx, _lane_pad(starts, 0), tm_moe)
        dest_flat = dest[:, :TOP_K].reshape(R * TOP_K)

        xb = _dispatch(n_valid, dest_flat, hp, m_pad, tm_moe)
        ybk = _ffn(block_exp + l * E, n_valid, n_used, xb, wg_all, bg_all, wl_all, bl_all, wd_all, bd_all, tm_moe)
        h, hb = _combine(dest_flat, gate, hn, ln2_g[l], ln2_b[l], ybk, alpha, tm_moe)

    return h.reshape(B, Tp, D)[:, pad + N_META:]
```

```python
import functools

import jax
import jax.numpy as jnp
from jax import lax
from jax.experimental import pallas as pl
from jax.experimental.pallas import tpu as pltpu

F32 = jnp.float32
BF16 = jnp.bfloat16
I32 = jnp.int32
U32 = jnp.uint32

N_META = 16
CONV_W = 4
CHUNK = 64
TOP_K = 4
RG_BLOCKS = 8
RG_C = 8.0
GDN_DK = 128
GDN_DV = 128
GDN_H_QK = 4
GDN_H_V = 8
GDN_QK_DIM = GDN_H_QK * GDN_DK
GDN_V_DIM = GDN_H_V * GDN_DV
SWIGLU_LIMIT = 7.0
SWIGLU_ALPHA = 1.702
LN_EPS = 1e-5
RMS_EPS = 1e-6

LANES = 128
SUBLANES = 8
MIB = 1024 * 1024
HIGHEST = lax.Precision.HIGHEST


def _pick_tile(n, target, mult=16):
    best = None
    for t in range(mult, min(n, target) + 1, mult):
        if n % t == 0:
            best = t
    assert best is not None, (n, target, mult)
    return best


def _params(sem, vmem_mib):
    return pltpu.CompilerParams(dimension_semantics=sem, vmem_limit_bytes=vmem_mib * MIB)


def _layer_norm(x, g, b):
    mu = jnp.mean(x, axis=-1, keepdims=True)
    xc = x - mu
    var = jnp.mean(xc * xc, axis=-1, keepdims=True)
    return xc * lax.rsqrt(var + LN_EPS) * g + b


def _sigmoid(x):
    return 1.0 / (1.0 + jnp.exp(-x))


def _softplus(x):
    return jnp.maximum(x, 0.0) + jnp.log1p(jnp.exp(-jnp.abs(x)))


def _silu(x):
    return x * _sigmoid(x)


def _gelu_tanh(x):
    return 0.5 * x * (1.0 + jnp.tanh(0.7978845608028654 * (x + 0.044715 * x * x * x)))


def _pack_pairs(x):
    n = x.shape[1] // 2
    xb = x.astype(BF16).astype(F32)
    lo = lax.bitcast_convert_type(xb[:, :n], U32) >> 16
    hi = lax.bitcast_convert_type(xb[:, n:], U32) & jnp.uint32(0xFFFF0000)
    return lo | hi


def _unpack_pairs(w):
    lo = lax.bitcast_convert_type(w << 16, F32)
    hi = lax.bitcast_convert_type(w & jnp.uint32(0xFFFF0000), F32)
    return jnp.concatenate([lo, hi], axis=1).astype(BF16)


def _ln_in_kernel(x_ref, g_ref, b_ref, of_ref, ob_ref):
    y = _layer_norm(x_ref[...], g_ref[...], b_ref[...])
    of_ref[...] = y
    ob_ref[...] = y.astype(BF16)


def _ln_in(x2d, g, b, tm):
    R, D = x2d.shape
    row = pl.BlockSpec((tm, D), lambda i: (i, 0))
    vec = pl.BlockSpec((1, D), lambda i: (0, 0))
    return pl.pallas_call(
        _ln_in_kernel,
        out_shape=(jax.ShapeDtypeStruct((R, D), F32), jax.ShapeDtypeStruct((R, D), BF16)),
        grid=(R // tm,),
        in_specs=[row, vec, vec],
        out_specs=(row, row),
        compiler_params=_params(("arbitrary",), 32),
        name="ln_in",
    )(x2d, g.reshape(1, D), b.reshape(1, D))


def _inproj_kernel(x_ref, w_ref, ws_ref, o_ref, os_ref):
    x = x_ref[...]
    o_ref[...] = jnp.dot(x, w_ref[...], preferred_element_type=F32)

    @pl.when(pl.program_id(1) == 0)
    def _():
        os_ref[...] = jnp.dot(x, ws_ref[...], preferred_element_type=F32)


def _inproj(hb, w_main, w_small, tm, tn):
    R, D = hb.shape
    N = w_main.shape[1]
    return pl.pallas_call(
        _inproj_kernel,
        out_shape=(jax.ShapeDtypeStruct((R, N), F32), jax.ShapeDtypeStruct((R, LANES), F32)),
        grid=(R // tm, N // tn),
        in_specs=[
            pl.BlockSpec((tm, D), lambda i, j: (i, 0)),
            pl.BlockSpec((D, tn), lambda i, j: (0, j)),
            pl.BlockSpec((D, LANES), lambda i, j: (0, 0)),
        ],
        out_specs=(
            pl.BlockSpec((tm, tn), lambda i, j: (i, j)),
            pl.BlockSpec((tm, LANES), lambda i, j: (i, 0)),
        ),
        compiler_params=_params(("arbitrary", "arbitrary"), 48),
        name="inproj",
    )(hb, w_main, w_small)


def _causal_conv(x, xbuf, cw_ref, tt):
    xbuf[SUBLANES:SUBLANES + tt, :] = x
    y = cw_ref[CONV_W - 1:CONV_W, :] * x
    for j in range(CONV_W - 1):
        off = SUBLANES - (CONV_W - 1) + j
        y = y + cw_ref[j:j + 1, :] * xbuf[off:off + tt, :]
    xbuf[0:SUBLANES, :] = xbuf[tt:tt + SUBLANES, :]
    return y


def _rglru_kernel(x_ref, y_ref, cw_ref, cb_ref, wx_ref, bx_ref, wa_ref, ba_ref, lam_ref,
                  o_ref, xbuf, a_s, u_s, hc, *, pad):
    t = pl.program_id(1)
    tt, C = x_ref.shape
    bs = C // RG_BLOCKS

    @pl.when(t == 0)
    def _():
        xbuf[0:SUBLANES, :] = jnp.zeros((SUBLANES, C), F32)
        hc[...] = jnp.zeros((SUBLANES, C), F32)

    row = lax.broadcasted_iota(I32, (tt, 1), 0)
    valid = (row + t * tt) >= pad
    x = jnp.where(valid, x_ref[...], 0.0)
    xc = _causal_conv(x, xbuf, cw_ref, tt) + cb_ref[...]

    xcb = xc.astype(BF16)
    gi = jnp.concatenate(
        [jnp.dot(xcb[:, h * bs:(h + 1) * bs], wx_ref[h], preferred_element_type=F32) for h in range(RG_BLOCKS)],
        axis=1)
    gr = jnp.concatenate(
        [jnp.dot(xcb[:, h * bs:(h + 1) * bs], wa_ref[h], preferred_element_type=F32) for h in range(RG_BLOCKS)],
        axis=1)
    gi = _sigmoid(gi + bx_ref[...])
    gr = _sigmoid(gr + ba_ref[...])
    log_a = (-RG_C) * gr * _softplus(-lam_ref[...])
    a = jnp.exp(log_a)
    u = jnp.sqrt(-jnp.tanh(log_a) * (a * a + 1.0)) * gi * xc
    u = jnp.where(valid, u, 0.0)

    rowmod = row % SUBLANES
    for d in (1, 2, 4):
        m = rowmod >= d
        a_sh = pltpu.roll(a, d, 0)
        u_sh = pltpu.roll(u, d, 0)
        u = jnp.where(m, a * u_sh + u, u)
        a = jnp.where(m, a * a_sh, a)
    a_s[...] = a
    u_s[...] = u

    def group(g, hprev):
        r0 = pl.multiple_of(g * SUBLANES, SUBLANES)
        h = a_s[pl.ds(r0, SUBLANES), :] * hprev + u_s[pl.ds(r0, SUBLANES), :]
        u_s[pl.ds(r0, SUBLANES), :] = h
        return jnp.broadcast_to(h[SUBLANES - 1:SUBLANES, :], (SUBLANES, C))

    hc[...] = lax.fori_loop(0, tt // SUBLANES, group, hc[...])
    o_ref[...] = (u_s[...] * _gelu_tanh(y_ref[...])).astype(BF16)


def _rglru(proj, cw, cb, wx, bx, wa, ba, lam, B, Tp, pad, tt, xcol, ycol):
    R = proj.shape[0]
    C = cw.shape[1]
    nT = Tp // tt
    vec = pl.BlockSpec((1, C), lambda b, t: (0, 0))
    wspec = pl.BlockSpec(wx.shape, lambda b, t: (0, 0, 0))
    return pl.pallas_call(
        functools.partial(_rglru_kernel, pad=pad),
        out_shape=jax.ShapeDtypeStruct((R, C), BF16),
        grid=(B, nT),
        in_specs=[
            pl.BlockSpec((tt, C), lambda b, t: (b * nT + t, xcol)),
            pl.BlockSpec((tt, C), lambda b, t: (b * nT + t, ycol)),
            pl.BlockSpec((CONV_W, C), lambda b, t: (0, 0)),
            vec, wspec, vec, wspec, vec, vec,
        ],
        out_specs=pl.BlockSpec((tt, C), lambda b, t: (b * nT + t, 0)),
        scratch_shapes=[
            pltpu.VMEM((tt + SUBLANES, C), F32),
            pltpu.VMEM((tt, C), F32),
            pltpu.VMEM((tt, C), F32),
            pltpu.VMEM((SUBLANES, C), F32),
        ],
        compiler_params=_params(("arbitrary", "arbitrary"), 48),
        name="rglru",
    )(proj, proj, cw, cb.reshape(1, C), wx, bx.reshape(1, C), wa, ba.reshape(1, C), lam.reshape(1, C))


def _bdot(a, b):
    return jnp.dot(a.astype(BF16), b.astype(BF16), preferred_element_type=F32)


def _bdot_nt(a, b):
    return lax.dot_general(a.astype(BF16), b.astype(BF16), (((1,), (1,)), ((), ())), preferred_element_type=F32)


def _bdot_tn(a, b):
    return lax.dot_general(a.astype(BF16), b.astype(BF16), (((0,), (0,)), ((), ())), preferred_element_type=F32)


def _unit_lower_inverses(Ls, masks):
    eye = masks[0]
    dinvs = [eye - L * masks[1] for L in Ls]
    for m in masks[2:]:
        ts = [_bdot(d, L * m) for d, L in zip(dinvs, Ls)]
        dinvs = [d - _bdot(t, d) for d, t in zip(dinvs, ts)]
    return dinvs


def _gdn_kernel(qkv_ref, z_ref, sm_ref, cw_ref, alog_ref, dtb_ref, nw_ref,
                o_ref, xbuf, q_s, k_s, v_s, b_s, g_s, S_ref, val_s, kq_s, at_s, kd_s, eg_s, *, pad):
    t = pl.program_id(1)
    tt, CQ = qkv_ref.shape
    ncht = tt // CHUNK
    rep = GDN_H_V // GDN_H_QK

    @pl.when(t == 0)
    def _():
        xbuf[0:SUBLANES, :] = jnp.zeros((SUBLANES, CQ), F32)
        S_ref[...] = jnp.zeros(S_ref.shape, F32)

    row = lax.broadcasted_iota(I32, (tt, 1), 0)
    valid = (row + t * tt) >= pad
    x = jnp.where(valid, qkv_ref[...], 0.0)
    xc = _silu(_causal_conv(x, xbuf, cw_ref, tt))

    for hq in range(GDN_H_QK):
        qh = xc[:, hq * GDN_DK:(hq + 1) * GDN_DK]
        kh = xc[:, GDN_QK_DIM + hq * GDN_DK:GDN_QK_DIM + (hq + 1) * GDN_DK]
        qn = qh * lax.rsqrt(jnp.sum(qh * qh, axis=-1, keepdims=True) + RMS_EPS) * (GDN_DK ** -0.5)
        kn = kh * lax.rsqrt(jnp.sum(kh * kh, axis=-1, keepdims=True) + RMS_EPS)
        q_s[:, hq * GDN_DK:(hq + 1) * GDN_DK] = qn
        k_s[:, hq * GDN_DK:(hq + 1) * GDN_DK] = kn
    v_s[...] = xc[:, 2 * GDN_QK_DIM:]

    sm = sm_ref[...]
    b_s[...] = _sigmoid(sm)
    g = -jnp.exp(alog_ref[...]) * _softplus(sm + dtb_ref[...])
    g_s[...] = jnp.where(valid, g, 0.0)

    ri = lax.broadcasted_iota(I32, (CHUNK, CHUNK), 0)
    ci = lax.broadcasted_iota(I32, (CHUNK, CHUNK), 1)
    causal = ri >= ci
    strict = ri > ci
    tril = causal.astype(F32)
    masks = [(ri == ci).astype(F32)]
    s = 1
    while s < CHUNK:
        masks.append(((ri // (2 * s) == ci // (2 * s)) & (ri % (2 * s) >= s) & (ci % (2 * s) < s)).astype(F32))
        s *= 2
    nw = nw_ref[...]

    heads = range(GDN_H_V)
    chunks = range(ncht)
    inst = [(c, h) for c in chunks for h in heads]

    rows = [slice(c * CHUNK, (c + 1) * CHUNK) for c in chunks]
    G = [jnp.dot(tril, g_s[rows[c], :], precision=HIGHEST, preferred_element_type=F32) for c in chunks]
    GT = [G[c].T for c in chunks]
    bet = [b_s[rows[c], :] for c in chunks]
    qs = [[q_s[rows[c], hq * GDN_DK:(hq + 1) * GDN_DK] for hq in range(GDN_H_QK)] for c in chunks]
    ks_ = [[k_s[rows[c], hq * GDN_DK:(hq + 1) * GDN_DK] for hq in range(GDN_H_QK)] for c in chunks]
    kk = [[_bdot_nt(k, k) for k in ks_[c]] for c in chunks]
    qk = [[_bdot_nt(q, k) for q, k in zip(qs[c], ks_[c])] for c in chunks]
    bcol = {(c, h): bet[c][:, h:h + 1] for c, h in inst}
    Gc = {(c, h): G[c][:, GDN_H_V + h:GDN_H_V + h + 1] for c, h in inst}
    decay = {(c, h): jnp.where(causal, jnp.exp(jnp.minimum(
        Gc[c, h] - GT[c][GDN_H_V + h:GDN_H_V + h + 1, :], 0.0)), 0.0) for c, h in inst}
    eG = {i: jnp.exp(Gc[i]) for i in inst}
    g_last = {i: Gc[i][CHUNK - 1:CHUNK, :] for i in inst}
    Ls = [jnp.where(strict, kk[c][h // rep] * bcol[c, h] * decay[c, h], 0.0) for c, h in inst]
    Ts = dict(zip(inst, _unit_lower_inverses(Ls, masks)))
    for c, h in inst:
        k = ks_[c][h // rep]
        sol = _bdot(Ts[c, h], jnp.concatenate(
            [v_s[rows[c], h * GDN_DV:(h + 1) * GDN_DV] * bcol[c, h], k * (bcol[c, h] * eG[c, h])], axis=1))
        val_s[c, h] = sol[:, :GDN_DV]
        kq_s[c, h] = jnp.concatenate([sol[:, GDN_DV:], qs[c][h // rep] * eG[c, h]], axis=0).astype(BF16)
        at_s[c, h] = (qk[c][h // rep] * decay[c, h]).astype(BF16)
        kd_s[c, h] = (k * jnp.exp(g_last[c, h] - Gc[c, h])).astype(BF16)
        eg_s[c, h] = jnp.broadcast_to(jnp.exp(g_last[c, h]), (SUBLANES, GDN_DV))

    for c in chunks:
        zc = z_ref[rows[c], :]
        Sold = [S_ref[h] for h in heads]
        PS = [_bdot(kq_s[c, h], Sold[h]) for h in heads]
        v_new = [val_s[c, h] - PS[h][:CHUNK] for h in heads]
        o = [PS[h][CHUNK:] + _bdot(at_s[c, h], v_new[h]) for h in heads]
        for h in heads:
            S_ref[h] = Sold[h] * eg_s[c, h][0:1, :] + _bdot_tn(kd_s[c, h], v_new[h])
        for h in heads:
            oh = o[h] * lax.rsqrt(jnp.mean(o[h] * o[h], axis=-1, keepdims=True) + RMS_EPS) * nw
            oh = oh * _silu(zc[:, h * GDN_DV:(h + 1) * GDN_DV])
            o_ref[rows[c], h * GDN_DV:(h + 1) * GDN_DV] = oh.astype(BF16)


def _gdn(proj, small, cw, alog_pad, dtb_pad, nw, B, Tp, pad, tt, qkvcol, zcol):
    R = proj.shape[0]
    CQ = cw.shape[1]
    nT = Tp // tt
    lane_vec = pl.BlockSpec((1, LANES), lambda b, t: (0, 0))
    return pl.pallas_call(
        functools.partial(_gdn_kernel, pad=pad),
        out_shape=jax.ShapeDtypeStruct((R, GDN_V_DIM), BF16),
        grid=(B, nT),
        in_specs=[
            pl.BlockSpec((tt, CQ), lambda b, t: (b * nT + t, qkvcol)),
            pl.BlockSpec((tt, GDN_V_DIM), lambda b, t: (b * nT + t, zcol)),
            pl.BlockSpec((tt, LANES), lambda b, t: (b * nT + t, 0)),
            pl.BlockSpec((CONV_W, CQ), lambda b, t: (0, 0)),
            lane_vec, lane_vec, lane_vec,
        ],
        out_specs=pl.BlockSpec((tt, GDN_V_DIM), lambda b, t: (b * nT + t, 0)),
        scratch_shapes=[
            pltpu.VMEM((tt + SUBLANES, CQ), F32),
            pltpu.VMEM((tt, GDN_QK_DIM), F32),
            pltpu.VMEM((tt, GDN_QK_DIM), F32),
            pltpu.VMEM((tt, GDN_V_DIM), F32),
            pltpu.VMEM((tt, LANES), F32),
            pltpu.VMEM((tt, LANES), F32),
            pltpu.VMEM((GDN_H_V, GDN_DK, GDN_DV), F32),
            pltpu.VMEM((tt // CHUNK, GDN_H_V, CHUNK, GDN_DV), F32),
            pltpu.VMEM((tt // CHUNK, GDN_H_V, 2 * CHUNK, GDN_DK), BF16),
            pltpu.VMEM((tt // CHUNK, GDN_H_V, CHUNK, CHUNK), BF16),
            pltpu.VMEM((tt // CHUNK, GDN_H_V, CHUNK, GDN_DK), BF16),
            pltpu.VMEM((tt // CHUNK, GDN_H_V, SUBLANES, GDN_DV), F32),
        ],
        compiler_params=_params(("arbitrary", "arbitrary"), 48),
        name="gdn",
    )(proj, proj, small, cw, alog_pad, dtb_pad, nw)


MERGE_SUBTILES = 2


def _merge_kernel(ya_ref, yb_ref, ga_ref, gb_ref, h_ref, wa_ref, wb_ref, wo_ref, g1_ref, b1_ref,
                  wr_ref, br_ref, hf_ref, hp_ref, idx_ref, gate_ref, cnt_ref, *, alpha):
    i = pl.program_id(0)
    tm, D = h_ref.shape
    sub = tm // MERGE_SUBTILES
    lane = lax.broadcasted_iota(I32, (sub, LANES), 1)
    counts = jnp.zeros((1, LANES), F32)
    for s in range(MERGE_SUBTILES):
        rows = slice(s * sub, (s + 1) * sub)
        ma = jnp.dot(ya_ref[rows, :], wa_ref[...], preferred_element_type=F32)
        mb = jnp.dot(yb_ref[rows, :], wb_ref[...], preferred_element_type=F32)
        mixed = _sigmoid(ga_ref[rows, :]) * ma + _sigmoid(gb_ref[rows, :]) * mb
        out = jnp.dot(mixed.astype(BF16), wo_ref[...], preferred_element_type=F32)
        hn = _layer_norm(alpha * h_ref[rows, :] + out, g1_ref[...], b1_ref[...])
        hf_ref[rows, :] = hn
        hp_ref[rows, :] = _pack_pairs(hn)

        hn_hi = hn.astype(BF16)
        hn_lo = (hn - hn_hi.astype(F32)).astype(BF16)
        p_hi = jnp.dot(hn_hi, wr_ref[...], preferred_element_type=F32)
        p_lo = jnp.dot(hn_lo, wr_ref[:, :LANES], preferred_element_type=F32)
        vals = p_hi[:, :LANES] + p_hi[:, LANES:] + p_lo + br_ref[...]
        tops, sels = [], []
        idx_out = jnp.zeros((sub, LANES), I32)
        for j in range(TOP_K):
            m = jnp.max(vals, axis=-1, keepdims=True)
            idx = jnp.min(jnp.where(vals == m, lane, LANES), axis=-1, keepdims=True)
            sel = lane == idx
            tops.append(m)
            sels.append(sel)
            idx_out = jnp.where(lane == j, idx, idx_out)
            vals = jnp.where(sel, -jnp.inf, vals)
        es = [jnp.exp(m - tops[0]) for m in tops]
        inv = 1.0 / (es[0] + es[1] + es[2] + es[3])
        gate_out = jnp.zeros((sub, LANES), F32)
        onehot = jnp.zeros((sub, LANES), F32)
        for j in range(TOP_K):
            gate_out = jnp.where(lane == j, es[j] * inv, gate_out)
            onehot = onehot + sels[j].astype(F32)
        idx_ref[rows, :] = idx_out
        gate_ref[rows, :] = gate_out
        counts = counts + jnp.sum(onehot, axis=0, keepdims=True)

    @pl.when(i == 0)
    def _():
        cnt_ref[...] = jnp.zeros(cnt_ref.shape, F32)

    cnt_ref[0:1, :] = cnt_ref[0:1, :] + counts


def _merge(ya, yb, proj, h, wa, wb, wo, g1, b1, wr, br, alpha, tm, gacol, gbcol):
    R, D = h.shape
    C = ya.shape[1]
    const2 = lambda i: (0, 0)
    single = pl.Buffered(1)
    wr_hi = wr.astype(BF16)
    wr_split = jnp.concatenate([wr_hi, (wr - wr_hi.astype(F32)).astype(BF16)], axis=1)
    return pl.pallas_call(
        functools.partial(_merge_kernel, alpha=alpha),
        out_shape=(
            jax.ShapeDtypeStruct((R, D), F32),
            jax.ShapeDtypeStruct((R, D // 2), U32),
            jax.ShapeDtypeStruct((R, LANES), I32),
            jax.ShapeDtypeStruct((R, LANES), F32),
            jax.ShapeDtypeStruct((SUBLANES, LANES), F32),
        ),
        grid=(R // tm,),
        in_specs=[
            pl.BlockSpec((tm, C), lambda i: (i, 0)),
            pl.BlockSpec((tm, C), lambda i: (i, 0)),
            pl.BlockSpec((tm, D), lambda i: (i, gacol)),
            pl.BlockSpec((tm, D), lambda i: (i, gbcol)),
            pl.BlockSpec((tm, D), lambda i: (i, 0)),
            pl.BlockSpec((C, D), const2, pipeline_mode=single),
            pl.BlockSpec((C, D), const2, pipeline_mode=single),
            pl.BlockSpec((D, D), const2, pipeline_mode=single),
            pl.BlockSpec((1, D), const2),
            pl.BlockSpec((1, D), const2),
            pl.BlockSpec((D, 2 * LANES), const2, pipeline_mode=single),
            pl.BlockSpec((1, LANES), const2),
        ],
        out_specs=(
            pl.BlockSpec((tm, D), lambda i: (i, 0)),
            pl.BlockSpec((tm, D // 2), lambda i: (i, 0)),
            pl.BlockSpec((tm, LANES), lambda i: (i, 0)),
            pl.BlockSpec((tm, LANES), lambda i: (i, 0)),
            pl.BlockSpec((SUBLANES, LANES), const2),
        ),
        compiler_params=_params(("arbitrary",), 56),
        name="merge",
    )(ya, yb, proj, proj, h, wa, wb, wo, g1.reshape(1, D), b1.reshape(1, D), wr_split, br)


def _dest_kernel(idx_ref, start_ref, dest_ref, carry):
    i = pl.program_id(0)
    tr = idx_ref.shape[0]

    @pl.when(i == 0)
    def _():
        carry[...] = jnp.zeros(carry.shape, F32)

    idx = idx_ref[...]
    lane = lax.broadcasted_iota(I32, (tr, LANES), 1)
    ohs = [(lane == idx[:, j:j + 1]).astype(F32) for j in range(TOP_K)]
    oh = ohs[0] + ohs[1] + ohs[2] + ohs[3]
    ri = lax.broadcasted_iota(I32, (tr, tr), 0)
    ci = lax.broadcasted_iota(I32, (tr, tr), 1)
    stril = (ri > ci).astype(BF16)
    before = jnp.dot(stril, oh.astype(BF16), preferred_element_type=F32) + carry[0:1, :] + start_ref[...]
    dest = jnp.zeros((tr, LANES), I32)
    for j in range(TOP_K):
        pos = jnp.sum(ohs[j] * before, axis=-1, keepdims=True)
        dest = jnp.where(lane == j, pos.astype(I32), dest)
    dest_ref[...] = dest
    carry[0:1, :] = carry[0:1, :] + jnp.sum(oh, axis=0, keepdims=True)


def _dest(idx, starts, tr):
    R = idx.shape[0]
    return pl.pallas_call(
        _dest_kernel,
        out_shape=jax.ShapeDtypeStruct((R, LANES), I32),
        grid=(R // tr,),
        in_specs=[pl.BlockSpec((tr, LANES), lambda i: (i, 0)), pl.BlockSpec((1, LANES), lambda i: (0, 0))],
        out_specs=pl.BlockSpec((tr, LANES), lambda i: (i, 0)),
        scratch_shapes=[pltpu.VMEM((SUBLANES, LANES), F32)],
        compiler_params=_params(("arbitrary",), 32),
        name="moe_dest",
    )(idx, starts)


ROW_UNROLL = 4


def _dispatch_kernel(nv_ref, dest_ref, x_ref, xb_ref, zbuf, sem, zsem):
    tm = x_ref.shape[0]
    bm = zbuf.shape[0]
    n_blocks = xb_ref.shape[0] // bm

    @pl.when(pl.program_id(0) == 0)
    def _():
        zbuf[...] = jnp.zeros(zbuf.shape, U32)

        def zero_copy(b):
            return pltpu.make_async_copy(zbuf, xb_ref.at[pl.ds(pl.multiple_of(b * bm, bm), bm), :], zsem)

        def start(b, c):
            @pl.when(nv_ref[b] < bm)
            def _():
                zero_copy(b).start()
            return c

        def wait(b, c):
            @pl.when(nv_ref[b] < bm)
            def _():
                zero_copy(b).wait()
            return c

        lax.fori_loop(0, n_blocks, start, 0)
        lax.fori_loop(0, n_blocks, wait, 0)

    def row(r, c):
        for j in range(TOP_K):
            d = dest_ref[r * TOP_K + j]
            pltpu.make_async_copy(x_ref.at[pl.ds(r, 1), :], xb_ref.at[pl.ds(d, 1), :], sem).start(priority=j % 2)
        return c

    lax.fori_loop(0, tm, row, 0, unroll=ROW_UNROLL)
    for j in range(TOP_K):
        pltpu.make_async_copy(x_ref, xb_ref.at[pl.ds(0, tm), :], sem).wait()


def _dispatch(n_valid, dest_flat, hp, m_pad, tm, bm):
    R, W = hp.shape
    grid_spec = pltpu.PrefetchScalarGridSpec(
        num_scalar_prefetch=1,
        grid=(R // tm,),
        in_specs=[
            pl.BlockSpec((tm * TOP_K,), lambda i, nv: (i,), memory_space=pltpu.SMEM),
            pl.BlockSpec((tm, W), lambda i, nv: (i, 0)),
        ],
        out_specs=pl.BlockSpec(memory_space=pl.ANY),
        scratch_shapes=[pltpu.VMEM((bm, W), U32), pltpu.SemaphoreType.DMA, pltpu.SemaphoreType.DMA],
    )
    return pl.pallas_call(
        _dispatch_kernel,
        out_shape=jax.ShapeDtypeStruct((m_pad, W), U32),
        grid_spec=grid_spec,
        compiler_params=_params(("arbitrary",), 32),
        name="moe_dispatch",
    )(n_valid, dest_flat, hp)


def _deint_kernel(w_ref, p_ref, g_ref, l_ref):
    w = w_ref[0].astype(BF16)
    perm = p_ref[...]
    width = perm.shape[0]
    half = width // 2
    for g in range(w.shape[1] // width):
        t = jnp.dot(w[:, g * width:(g + 1) * width], perm, preferred_element_type=F32)
        g_ref[0, :, g * half:(g + 1) * half] = t[:, :half].astype(BF16)
        l_ref[0, :, g * half:(g + 1) * half] = t[:, half:].astype(BF16)


def _deinterleave(w_gu, tr):
    N, D, DE2 = w_gu.shape
    DE = DE2 // 2
    width = 2 * LANES
    src = lax.broadcasted_iota(I32, (width, width), 0)
    dst = lax.broadcasted_iota(I32, (width, width), 1)
    perm = (src == jnp.where(dst < LANES, 2 * dst, 2 * (dst - LANES) + 1)).astype(BF16)
    out = jax.ShapeDtypeStruct((N, D, DE), BF16)
    return pl.pallas_call(
        _deint_kernel,
        out_shape=(out, out),
        grid=(N, D // tr),
        in_specs=[
            pl.BlockSpec((1, tr, DE2), lambda n, i: (n, i, 0)),
            pl.BlockSpec((width, width), lambda n, i: (0, 0)),
        ],
        out_specs=(pl.BlockSpec((1, tr, DE), lambda n, i: (n, i, 0)),
                   pl.BlockSpec((1, tr, DE), lambda n, i: (n, i, 0))),
        compiler_params=_params(("arbitrary", "arbitrary"), 40),
        name="deinterleave",
    )(w_gu, perm)


def _ffn_kernel(be_ref, nv_ref, nu_ref, x_ref, wg_ref, bg_ref, wl_ref, bl_ref, wd_ref, bd_ref, o_ref):
    del be_ref
    b = pl.program_id(0)
    tm = x_ref.shape[0]
    half = tm // 2
    used = b < nu_ref[0]
    n_valid = nv_ref[b]

    def ffn_rows(rows):
        x = _unpack_pairs(x_ref[rows, :])
        glu = jnp.dot(x, wg_ref[0], preferred_element_type=F32) + bg_ref[0]
        lin = jnp.dot(x, wl_ref[0], preferred_element_type=F32) + bl_ref[0]
        glu = jnp.minimum(glu, SWIGLU_LIMIT)
        lin = jnp.clip(lin, -SWIGLU_LIMIT, SWIGLU_LIMIT)
        act = glu * _sigmoid(SWIGLU_ALPHA * glu) * (lin + 1.0)
        y = jnp.dot(act.astype(BF16), wd_ref[0], preferred_element_type=F32) + bd_ref[0]
        o_ref[rows, :] = _pack_pairs(y)

    @pl.when(jnp.logical_not(used))
    def _():
        o_ref[...] = jnp.zeros(o_ref.shape, U32)

    @pl.when(jnp.logical_and(used, n_valid > half))
    def _():
        ffn_rows(slice(0, tm))

    @pl.when(jnp.logical_and(used, n_valid <= half))
    def _():
        ffn_rows(slice(0, half))
        o_ref[half:, :] = jnp.zeros((tm - half, o_ref.shape[1]), U32)


def _ffn(block_exp, n_valid, n_used, xb, wg, bg, wl, bl, wd, bd, tm):
    m_pad, W = xb.shape
    _, D, DE = wg.shape
    n_blocks = m_pad // tm

    def rows(b, be, nv, nu):
        return (jnp.minimum(b, nu[0] - 1), 0)

    def expert(b, be, nv, nu):
        return (be[jnp.minimum(b, nu[0] - 1)], 0, 0)

    grid_spec = pltpu.PrefetchScalarGridSpec(
        num_scalar_prefetch=3,
        grid=(n_blocks,),
        in_specs=[
            pl.BlockSpec((tm, W), rows),
            pl.BlockSpec((1, D, DE), expert),
            pl.BlockSpec((1, 1, DE), expert),
            pl.BlockSpec((1, D, DE), expert),
            pl.BlockSpec((1, 1, DE), expert),
            pl.BlockSpec((1, DE, D), expert),
            pl.BlockSpec((1, 1, D), expert),
        ],
        out_specs=pl.BlockSpec((tm, W), lambda b, be, nv, nu: (b, 0)),
    )
    return pl.pallas_call(
        _ffn_kernel,
        out_shape=jax.ShapeDtypeStruct((m_pad, W), U32),
        grid_spec=grid_spec,
        compiler_params=_params(("arbitrary",), 56),
        name="moe_ffn",
    )(block_exp, n_valid, n_used, xb, wg, bg, wl, bl, wd, bd)


def _combine_kernel(dest_ref, dnext_ref, gate_ref, h_ref, g2_ref, b2_ref, yb_ref, of_ref, ob_ref, buf, sems,
                    *, alpha):
    i = pl.program_id(0)
    tm, D = h_ref.shape
    slot = i % 2

    def gather(d_ref, s):
        def row(r, c):
            for j in range(TOP_K):
                d = d_ref[r * TOP_K + j]
                pltpu.make_async_copy(
                    yb_ref.at[pl.ds(d, 1), :], buf.at[s, j, pl.ds(r, 1), :], sems.at[s]).start(priority=j % 2)
            return c

        lax.fori_loop(0, tm, row, 0, unroll=ROW_UNROLL)

    @pl.when(i == 0)
    def _():
        gather(dest_ref, slot)

    @pl.when(i + 1 < pl.num_programs(0))
    def _():
        gather(dnext_ref, 1 - slot)

    for j in range(TOP_K):
        pltpu.make_async_copy(yb_ref.at[pl.ds(0, tm), :], buf.at[slot, j], sems.at[slot]).wait()

    gate = gate_ref[...]
    moe = None
    for j in range(TOP_K):
        term = gate[:, j:j + 1] * _unpack_pairs(buf[slot, j]).astype(F32)
        moe = term if moe is None else moe + term
    y = _layer_norm(alpha * h_ref[...] + moe, g2_ref[...], b2_ref[...])
    of_ref[...] = y
    ob_ref[...] = y.astype(BF16)


def _combine(dest_flat, gate, h, g2, b2, yb, alpha, tm):
    R, D = h.shape
    W = yb.shape[1]
    n = R // tm
    row = pl.BlockSpec((tm, D), lambda i: (i, 0))
    vec = pl.BlockSpec((1, D), lambda i: (0, 0))
    return pl.pallas_call(
        functools.partial(_combine_kernel, alpha=alpha),
        out_shape=(jax.ShapeDtypeStruct((R, D), F32), jax.ShapeDtypeStruct((R, D), BF16)),
        grid=(n,),
        in_specs=[
            pl.BlockSpec((tm * TOP_K,), lambda i: (i,), memory_space=pltpu.SMEM),
            pl.BlockSpec((tm * TOP_K,), lambda i: (jnp.minimum(i + 1, n - 1),), memory_space=pltpu.SMEM),
            pl.BlockSpec((tm, LANES), lambda i: (i, 0)),
            row, vec, vec,
            pl.BlockSpec(memory_space=pl.ANY),
        ],
        out_specs=(row, row),
        scratch_shapes=[pltpu.VMEM((2, TOP_K, tm, W), U32), pltpu.SemaphoreType.DMA((2,))],
        compiler_params=_params(("arbitrary",), 48),
        name="moe_combine",
    )(dest_flat, dest_flat, gate, h, g2.reshape(1, D), b2.reshape(1, D), yb)


def _lane_pad(v, offset):
    return jnp.zeros((1, LANES), F32).at[0, offset:offset + v.shape[0]].set(v.astype(F32))


def kernel(x, meta_tokens, ln_in_g, ln_in_b, w_in, rg_conv_w, rg_conv_b, rg_wx, rg_bx, rg_wa, rg_ba, rg_lambda, gdn_conv_w, gdn_a_log, gdn_dt_bias, gdn_norm_w, w_branch_a, w_branch_b, w_out, ln1_g, ln1_b, w_router, b_router, w_gate_up, b_gate_up, w_down, b_down, ln2_g, ln2_b):
    B, seq, D = x.shape
    depth = w_in.shape[0]
    E = w_router.shape[2]
    DE = w_down.shape[2]
    d_rnn = D // 2
    assert D == 2 * GDN_V_DIM and d_rnn == GDN_V_DIM and E <= LANES
    alpha = float((2 * depth) ** 0.25)

    T = N_META + seq
    pad = (-T) % CHUNK
    Tp = T + pad
    R = B * Tp

    tm_ln = _pick_tile(R, 512)
    tm_in = _pick_tile(R, 1040)
    tn_in = 1024
    tt_rg = _pick_tile(Tp, 416, SUBLANES)
    tt_gdn = _pick_tile(Tp, 320, CHUNK)
    tm_mg = _pick_tile(R, 256)
    tm_moe = _pick_tile(R, 256)
    bm_ffn = 2 * tm_moe
    n_blocks = -(-(R * TOP_K) // bm_ffn) + E
    m_pad = n_blocks * bm_ffn

    meta = jnp.broadcast_to(meta_tokens[None].astype(x.dtype), (B, N_META, D))
    h0 = jnp.concatenate([jnp.zeros((B, pad, D), x.dtype), meta, x], axis=1).reshape(R, D)
    h, hb = _ln_in(h0, ln_in_g, ln_in_b, tm_ln)

    wg_all, wl_all = _deinterleave(w_gate_up.reshape(depth * E, D, 2 * DE), _pick_tile(D, 512))
    wd_all = w_down.reshape(depth * E, DE, D).astype(BF16)
    bg_all = b_gate_up.reshape(depth * E, 1, 2 * DE)[:, :, 0::2]
    bl_all = b_gate_up.reshape(depth * E, 1, 2 * DE)[:, :, 1::2]
    bd_all = b_down.reshape(depth * E, 1, D)

    o_rx = 0
    o_q = 2 * d_rnn
    o_z = o_q + 2 * GDN_QK_DIM + GDN_V_DIM
    o_beta = o_z + GDN_V_DIM
    o_ga = o_beta + 2 * GDN_H_V
    o_gb = o_ga + D
    block_start = jnp.arange(n_blocks, dtype=I32) * bm_ffn

    for l in range(depth):
        wl = w_in[l]
        w_main = jnp.concatenate(
            [wl[:, o_ga:o_ga + D], wl[:, o_gb:o_gb + D], wl[:, o_q:o_z], wl[:, o_rx:o_q], wl[:, o_z:o_beta]],
            axis=1).astype(BF16)
        w_small = jnp.zeros((D, LANES), F32).at[:, :2 * GDN_H_V].set(wl[:, o_beta:o_ga]).astype(BF16)
        proj, small = _inproj(hb, w_main, w_small, tm_in, tn_in)

        ya = _rglru(proj, rg_conv_w[l], rg_conv_b[l], rg_wx[l].astype(BF16), rg_bx[l], rg_wa[l].astype(BF16),
                    rg_ba[l], rg_lambda[l], B, Tp, pad, tt_rg, xcol=(2 * D + 2048) // d_rnn,
                    ycol=(2 * D + 2048) // d_rnn + 1)
        yb = _gdn(proj, small, gdn_conv_w[l], _lane_pad(gdn_a_log[l], GDN_H_V), _lane_pad(gdn_dt_bias[l], GDN_H_V),
                  gdn_norm_w[l].reshape(1, GDN_DV).astype(F32), B, Tp, pad, tt_gdn, qkvcol=2,
                  zcol=(2 * D + 2048 + 2 * d_rnn) // GDN_V_DIM)

        wr = jnp.zeros((D, LANES), F32).at[:, :E].set(w_router[l])
        br = jnp.full((1, LANES), -jnp.inf, F32).at[0, :E].set(b_router[l])
        hn, hp, idx, gate, counts = _merge(
            ya, yb, proj, h, w_branch_a[l].astype(BF16), w_branch_b[l].astype(BF16), w_out[l].astype(BF16),
            ln1_g[l], ln1_b[l], wr, br, alpha, tm_mg, gacol=0, gbcol=1)

        cnt = counts[0, :E].astype(I32)
        padded = (cnt + bm_ffn - 1) // bm_ffn * bm_ffn
        pad_ends = jnp.cumsum(padded)
        starts = pad_ends - padded
        block_exp = jnp.minimum(jnp.sum((pad_ends[None, :] <= block_start[:, None]).astype(I32), axis=1), E - 1)
        onehot = (block_exp[:, None] == jnp.arange(E, dtype=I32)[None, :]).astype(I32)
        n_valid = jnp.clip(jnp.sum(onehot * (cnt + starts)[None, :], axis=1) - block_start, 0, bm_ffn).astype(I32)
        n_used = (pad_ends[E - 1:E] // bm_ffn).astype(I32)

        dest = _dest(idx, _lane_pad(starts, 0), tm_moe)
        dest_flat = dest[:, :TOP_K].reshape(R * TOP_K)

        xb = _dispatch(n_valid, dest_flat, hp, m_pad, tm_moe, bm_ffn)
        ybk = _ffn(block_exp + l * E, n_valid, n_used, xb, wg_all, bg_all, wl_all, bl_all, wd_all, bd_all, bm_ffn)
        h, hb = _combine(dest_flat, gate, hn, ln2_g[l], ln2_b[l], ybk, alpha, tm_moe)

    return h.reshape(B, Tp, D)[:, pad + N_META:]
```

```python
import functools

import jax
import jax.numpy as jnp
from jax import lax
from jax.experimental import pallas as pl
from jax.experimental.pallas import tpu as pltpu

F32 = jnp.float32
BF16 = jnp.bfloat16
I32 = jnp.int32
U32 = jnp.uint32

N_META = 16
CONV_W = 4
CHUNK = 64
TOP_K = 4
RG_BLOCKS = 8
RG_C = 8.0
GDN_DK = 128
GDN_DV = 128
GDN_H_QK = 4
GDN_H_V = 8
GDN_QK_DIM = GDN_H_QK * GDN_DK
GDN_V_DIM = GDN_H_V * GDN_DV
SWIGLU_LIMIT = 7.0
SWIGLU_ALPHA = 1.702
LN_EPS = 1e-5
RMS_EPS = 1e-6

LANES = 128
SUBLANES = 8
MIB = 1024 * 1024
HIGHEST = lax.Precision.HIGHEST


def _pick_tile(n, target, mult=16):
    best = None
    for t in range(mult, min(n, target) + 1, mult):
        if n % t == 0:
            best = t
    assert best is not None, (n, target, mult)
    return best


def _params(sem, vmem_mib):
    return pltpu.CompilerParams(dimension_semantics=sem, vmem_limit_bytes=vmem_mib * MIB)


def _layer_norm(x, g, b):
    mu = jnp.mean(x, axis=-1, keepdims=True)
    xc = x - mu
    var = jnp.mean(xc * xc, axis=-1, keepdims=True)
    return xc * lax.rsqrt(var + LN_EPS) * g + b


def _sigmoid(x):
    return 1.0 / (1.0 + jnp.exp(-x))


def _softplus(x):
    return jnp.maximum(x, 0.0) + jnp.log1p(jnp.exp(-jnp.abs(x)))


def _silu(x):
    return x * _sigmoid(x)


def _gelu_tanh(x):
    return 0.5 * x * (1.0 + jnp.tanh(0.7978845608028654 * (x + 0.044715 * x * x * x)))


def _pack_pairs(x):
    n = x.shape[1] // 2
    xb = x.astype(BF16).astype(F32)
    lo = lax.bitcast_convert_type(xb[:, :n], U32) >> 16
    hi = lax.bitcast_convert_type(xb[:, n:], U32) & jnp.uint32(0xFFFF0000)
    return lo | hi


def _unpack_pairs_f32(w):
    lo = lax.bitcast_convert_type(w << 16, F32)
    hi = lax.bitcast_convert_type(w & jnp.uint32(0xFFFF0000), F32)
    return jnp.concatenate([lo, hi], axis=1)


def _unpack_pairs(w):
    return _unpack_pairs_f32(w).astype(BF16)


def _ln_in_kernel(x_ref, g_ref, b_ref, of_ref, ob_ref):
    y = _layer_norm(x_ref[...], g_ref[...], b_ref[...])
    of_ref[...] = y
    ob_ref[...] = y.astype(BF16)


def _ln_in(x2d, g, b, tm):
    R, D = x2d.shape
    row = pl.BlockSpec((tm, D), lambda i: (i, 0))
    vec = pl.BlockSpec((1, D), lambda i: (0, 0))
    return pl.pallas_call(
        _ln_in_kernel,
        out_shape=(jax.ShapeDtypeStruct((R, D), F32), jax.ShapeDtypeStruct((R, D), BF16)),
        grid=(R // tm,),
        in_specs=[row, vec, vec],
        out_specs=(row, row),
        compiler_params=_params(("arbitrary",), 32),
        name="ln_in",
    )(x2d, g.reshape(1, D), b.reshape(1, D))


def _inproj_kernel(x_ref, w_ref, ws_ref, o_ref, os_ref):
    x = x_ref[...]
    o_ref[...] = jnp.dot(x, w_ref[...], preferred_element_type=F32)

    @pl.when(pl.program_id(1) == 0)
    def _():
        os_ref[...] = jnp.dot(x, ws_ref[...], preferred_element_type=F32)


def _inproj(hb, w_main, w_small, tm, tn):
    R, D = hb.shape
    N = w_main.shape[1]
    return pl.pallas_call(
        _inproj_kernel,
        out_shape=(jax.ShapeDtypeStruct((R, N), F32), jax.ShapeDtypeStruct((R, LANES), F32)),
        grid=(R // tm, N // tn),
        in_specs=[
            pl.BlockSpec((tm, D), lambda i, j: (i, 0)),
            pl.BlockSpec((D, tn), lambda i, j: (0, j)),
            pl.BlockSpec((D, LANES), lambda i, j: (0, 0)),
        ],
        out_specs=(
            pl.BlockSpec((tm, tn), lambda i, j: (i, j)),
            pl.BlockSpec((tm, LANES), lambda i, j: (i, 0)),
        ),
        compiler_params=_params(("arbitrary", "arbitrary"), 48),
        name="inproj",
    )(hb, w_main, w_small)


def _causal_conv(x, xbuf, cw_ref, tt):
    xbuf[SUBLANES:SUBLANES + tt, :] = x
    y = cw_ref[CONV_W - 1:CONV_W, :] * x
    for j in range(CONV_W - 1):
        off = SUBLANES - (CONV_W - 1) + j
        y = y + cw_ref[j:j + 1, :] * xbuf[off:off + tt, :]
    xbuf[0:SUBLANES, :] = xbuf[tt:tt + SUBLANES, :]
    return y


def _rglru_kernel(x_ref, y_ref, cw_ref, cb_ref, wx_ref, bx_ref, wa_ref, ba_ref, lam_ref,
                  o_ref, xbuf, a_s, u_s, hc, *, pad):
    t = pl.program_id(1)
    tt, C = x_ref.shape
    bs = C // RG_BLOCKS

    @pl.when(t == 0)
    def _():
        xbuf[0:SUBLANES, :] = jnp.zeros((SUBLANES, C), F32)
        hc[...] = jnp.zeros((SUBLANES, C), F32)

    row = lax.broadcasted_iota(I32, (tt, 1), 0)
    valid = (row + t * tt) >= pad
    x = jnp.where(valid, x_ref[...], 0.0)
    xc = _causal_conv(x, xbuf, cw_ref, tt) + cb_ref[...]

    xcb = xc.astype(BF16)
    gi = jnp.concatenate(
        [jnp.dot(xcb[:, h * bs:(h + 1) * bs], wx_ref[h], preferred_element_type=F32) for h in range(RG_BLOCKS)],
        axis=1)
    gr = jnp.concatenate(
        [jnp.dot(xcb[:, h * bs:(h + 1) * bs], wa_ref[h], preferred_element_type=F32) for h in range(RG_BLOCKS)],
        axis=1)
    gi = _sigmoid(gi + bx_ref[...])
    gr = _sigmoid(gr + ba_ref[...])
    log_a = (-RG_C) * gr * _softplus(-lam_ref[...])
    a = jnp.exp(log_a)
    u = jnp.sqrt(-jnp.tanh(log_a) * (a * a + 1.0)) * gi * xc
    u = jnp.where(valid, u, 0.0)

    rowmod = row % SUBLANES
    for d in (1, 2, 4):
        m = rowmod >= d
        a_sh = pltpu.roll(a, d, 0)
        u_sh = pltpu.roll(u, d, 0)
        u = jnp.where(m, a * u_sh + u, u)
        a = jnp.where(m, a * a_sh, a)
    a_s[...] = a
    u_s[...] = u

    def group(g, hprev):
        r0 = pl.multiple_of(g * SUBLANES, SUBLANES)
        h = a_s[pl.ds(r0, SUBLANES), :] * hprev + u_s[pl.ds(r0, SUBLANES), :]
        u_s[pl.ds(r0, SUBLANES), :] = h
        return jnp.broadcast_to(h[SUBLANES - 1:SUBLANES, :], (SUBLANES, C))

    hc[...] = lax.fori_loop(0, tt // SUBLANES, group, hc[...])
    o_ref[...] = (u_s[...] * _gelu_tanh(y_ref[...])).astype(BF16)


def _rglru(proj, cw, cb, wx, bx, wa, ba, lam, B, Tp, pad, tt, xcol, ycol):
    R = proj.shape[0]
    C = cw.shape[1]
    nT = Tp // tt
    vec = pl.BlockSpec((1, C), lambda b, t: (0, 0))
    wspec = pl.BlockSpec(wx.shape, lambda b, t: (0, 0, 0))
    return pl.pallas_call(
        functools.partial(_rglru_kernel, pad=pad),
        out_shape=jax.ShapeDtypeStruct((R, C), BF16),
        grid=(B, nT),
        in_specs=[
            pl.BlockSpec((tt, C), lambda b, t: (b * nT + t, xcol)),
            pl.BlockSpec((tt, C), lambda b, t: (b * nT + t, ycol)),
            pl.BlockSpec((CONV_W, C), lambda b, t: (0, 0)),
            vec, wspec, vec, wspec, vec, vec,
        ],
        out_specs=pl.BlockSpec((tt, C), lambda b, t: (b * nT + t, 0)),
        scratch_shapes=[
            pltpu.VMEM((tt + SUBLANES, C), F32),
            pltpu.VMEM((tt, C), F32),
            pltpu.VMEM((tt, C), F32),
            pltpu.VMEM((SUBLANES, C), F32),
        ],
        compiler_params=_params(("arbitrary", "arbitrary"), 48),
        name="rglru",
    )(proj, proj, cw, cb.reshape(1, C), wx, bx.reshape(1, C), wa, ba.reshape(1, C), lam.reshape(1, C))


def _bdot(a, b):
    return jnp.dot(a.astype(BF16), b.astype(BF16), preferred_element_type=F32)


def _bdot_nt(a, b):
    return lax.dot_general(a.astype(BF16), b.astype(BF16), (((1,), (1,)), ((), ())), preferred_element_type=F32)


def _bdot_tn(a, b):
    return lax.dot_general(a.astype(BF16), b.astype(BF16), (((0,), (0,)), ((), ())), preferred_element_type=F32)


def _unit_lower_inverses(Ls, masks):
    eye = masks[0]
    dinvs = [eye - L * masks[1] for L in Ls]
    for m in masks[2:]:
        ts = [_bdot(d, L * m) for d, L in zip(dinvs, Ls)]
        dinvs = [d - _bdot(t, d) for d, t in zip(dinvs, ts)]
    return dinvs


def _gdn_kernel(qkv_ref, z_ref, sm_ref, cw_ref, alog_ref, dtb_ref, nw_ref,
                o_ref, xbuf, q_s, k_s, v_s, b_s, g_s, S_ref, val_s, kq_s, at_s, kd_s, eg_s, *, pad):
    t = pl.program_id(1)
    tt, CQ = qkv_ref.shape
    ncht = tt // CHUNK
    rep = GDN_H_V // GDN_H_QK

    @pl.when(t == 0)
    def _():
        xbuf[0:SUBLANES, :] = jnp.zeros((SUBLANES, CQ), F32)
        S_ref[...] = jnp.zeros(S_ref.shape, F32)

    row = lax.broadcasted_iota(I32, (tt, 1), 0)
    valid = (row + t * tt) >= pad
    x = jnp.where(valid, qkv_ref[...], 0.0)
    xc = _silu(_causal_conv(x, xbuf, cw_ref, tt))

    for hq in range(GDN_H_QK):
        qh = xc[:, hq * GDN_DK:(hq + 1) * GDN_DK]
        kh = xc[:, GDN_QK_DIM + hq * GDN_DK:GDN_QK_DIM + (hq + 1) * GDN_DK]
        qn = qh * lax.rsqrt(jnp.sum(qh * qh, axis=-1, keepdims=True) + RMS_EPS) * (GDN_DK ** -0.5)
        kn = kh * lax.rsqrt(jnp.sum(kh * kh, axis=-1, keepdims=True) + RMS_EPS)
        q_s[:, hq * GDN_DK:(hq + 1) * GDN_DK] = qn
        k_s[:, hq * GDN_DK:(hq + 1) * GDN_DK] = kn
    v_s[...] = xc[:, 2 * GDN_QK_DIM:]

    sm = sm_ref[...]
    b_s[...] = _sigmoid(sm)
    g = -jnp.exp(alog_ref[...]) * _softplus(sm + dtb_ref[...])
    g_s[...] = jnp.where(valid, g, 0.0)

    ri = lax.broadcasted_iota(I32, (CHUNK, CHUNK), 0)
    ci = lax.broadcasted_iota(I32, (CHUNK, CHUNK), 1)
    causal = ri >= ci
    strict = ri > ci
    tril = causal.astype(F32)
    masks = [(ri == ci).astype(F32)]
    s = 1
    while s < CHUNK:
        masks.append(((ri // (2 * s) == ci // (2 * s)) & (ri % (2 * s) >= s) & (ci % (2 * s) < s)).astype(F32))
        s *= 2
    nw = nw_ref[...]

    heads = range(GDN_H_V)
    chunks = range(ncht)
    inst = [(c, h) for c in chunks for h in heads]

    rows = [slice(c * CHUNK, (c + 1) * CHUNK) for c in chunks]
    G = [jnp.dot(tril, g_s[rows[c], :], precision=HIGHEST, preferred_element_type=F32) for c in chunks]
    GT = [G[c].T for c in chunks]
    bet = [b_s[rows[c], :] for c in chunks]
    qs = [[q_s[rows[c], hq * GDN_DK:(hq + 1) * GDN_DK] for hq in range(GDN_H_QK)] for c in chunks]
    ks_ = [[k_s[rows[c], hq * GDN_DK:(hq + 1) * GDN_DK] for hq in range(GDN_H_QK)] for c in chunks]
    kk = [[_bdot_nt(k, k) for k in ks_[c]] for c in chunks]
    qk = [[_bdot_nt(q, k) for q, k in zip(qs[c], ks_[c])] for c in chunks]
    bcol = {(c, h): bet[c][:, h:h + 1] for c, h in inst}
    Gc = {(c, h): G[c][:, GDN_H_V + h:GDN_H_V + h + 1] for c, h in inst}
    decay = {(c, h): jnp.where(causal, jnp.exp(jnp.minimum(
        Gc[c, h] - GT[c][GDN_H_V + h:GDN_H_V + h + 1, :], 0.0)), 0.0) for c, h in inst}
    eG = {i: jnp.exp(Gc[i]) for i in inst}
    g_last = {i: Gc[i][CHUNK - 1:CHUNK, :] for i in inst}
    Ls = [jnp.where(strict, kk[c][h // rep] * bcol[c, h] * decay[c, h], 0.0) for c, h in inst]
    Ts = dict(zip(inst, _unit_lower_inverses(Ls, masks)))
    for c, h in inst:
        k = ks_[c][h // rep]
        sol = _bdot(Ts[c, h], jnp.concatenate(
            [v_s[rows[c], h * GDN_DV:(h + 1) * GDN_DV] * bcol[c, h], k * (bcol[c, h] * eG[c, h])], axis=1))
        val_s[c, h] = sol[:, :GDN_DV]
        kq_s[c, h] = jnp.concatenate([sol[:, GDN_DV:], qs[c][h // rep] * eG[c, h]], axis=0).astype(BF16)
        at_s[c, h] = (qk[c][h // rep] * decay[c, h]).astype(BF16)
        kd_s[c, h] = (k * jnp.exp(g_last[c, h] - Gc[c, h])).astype(BF16)
        eg_s[c, h] = jnp.broadcast_to(jnp.exp(g_last[c, h]), (SUBLANES, GDN_DV))

    for c in chunks:
        zc = z_ref[rows[c], :]
        Sold = [S_ref[h] for h in heads]
        PS = [_bdot(kq_s[c, h], Sold[h]) for h in heads]
        v_new = [val_s[c, h] - PS[h][:CHUNK] for h in heads]
        o = [PS[h][CHUNK:] + _bdot(at_s[c, h], v_new[h]) for h in heads]
        for h in heads:
            S_ref[h] = Sold[h] * eg_s[c, h][0:1, :] + _bdot_tn(kd_s[c, h], v_new[h])
        for h in heads:
            oh = o[h] * lax.rsqrt(jnp.mean(o[h] * o[h], axis=-1, keepdims=True) + RMS_EPS) * nw
            oh = oh * _silu(zc[:, h * GDN_DV:(h + 1) * GDN_DV])
            o_ref[rows[c], h * GDN_DV:(h + 1) * GDN_DV] = oh.astype(BF16)


def _gdn(proj, small, cw, alog_pad, dtb_pad, nw, B, Tp, pad, tt, qkvcol, zcol):
    R = proj.shape[0]
    CQ = cw.shape[1]
    nT = Tp // tt
    lane_vec = pl.BlockSpec((1, LANES), lambda b, t: (0, 0))
    return pl.pallas_call(
        functools.partial(_gdn_kernel, pad=pad),
        out_shape=jax.ShapeDtypeStruct((R, GDN_V_DIM), BF16),
        grid=(B, nT),
        in_specs=[
            pl.BlockSpec((tt, CQ), lambda b, t: (b * nT + t, qkvcol)),
            pl.BlockSpec((tt, GDN_V_DIM), lambda b, t: (b * nT + t, zcol)),
            pl.BlockSpec((tt, LANES), lambda b, t: (b * nT + t, 0)),
            pl.BlockSpec((CONV_W, CQ), lambda b, t: (0, 0)),
            lane_vec, lane_vec, lane_vec,
        ],
        out_specs=pl.BlockSpec((tt, GDN_V_DIM), lambda b, t: (b * nT + t, 0)),
        scratch_shapes=[
            pltpu.VMEM((tt + SUBLANES, CQ), F32),
            pltpu.VMEM((tt, GDN_QK_DIM), F32),
            pltpu.VMEM((tt, GDN_QK_DIM), F32),
            pltpu.VMEM((tt, GDN_V_DIM), F32),
            pltpu.VMEM((tt, LANES), F32),
            pltpu.VMEM((tt, LANES), F32),
            pltpu.VMEM((GDN_H_V, GDN_DK, GDN_DV), F32),
            pltpu.VMEM((tt // CHUNK, GDN_H_V, CHUNK, GDN_DV), F32),
            pltpu.VMEM((tt // CHUNK, GDN_H_V, 2 * CHUNK, GDN_DK), BF16),
            pltpu.VMEM((tt // CHUNK, GDN_H_V, CHUNK, CHUNK), BF16),
            pltpu.VMEM((tt // CHUNK, GDN_H_V, CHUNK, GDN_DK), BF16),
            pltpu.VMEM((tt // CHUNK, GDN_H_V, SUBLANES, GDN_DV), F32),
        ],
        compiler_params=_params(("arbitrary", "arbitrary"), 48),
        name="gdn",
    )(proj, proj, small, cw, alog_pad, dtb_pad, nw)


MERGE_SUBTILES = 2


def _merge_kernel(ya_ref, yb_ref, ga_ref, gb_ref, h_ref, wa_ref, wb_ref, wo_ref, g1_ref, b1_ref,
                  wr_ref, br_ref, hf_ref, hp_ref, idx_ref, gate_ref, cnt_ref, *, alpha):
    i = pl.program_id(0)
    tm, D = h_ref.shape
    sub = tm // MERGE_SUBTILES
    lane = lax.broadcasted_iota(I32, (sub, LANES), 1)
    counts = jnp.zeros((1, LANES), F32)
    subs = range(MERGE_SUBTILES)
    rows_of = [slice(s * sub, (s + 1) * sub) for s in subs]
    branch = [(jnp.dot(ya_ref[r, :], wa_ref[...], preferred_element_type=F32),
               jnp.dot(yb_ref[r, :], wb_ref[...], preferred_element_type=F32)) for r in rows_of]
    outs = []
    for s in subs:
        r = rows_of[s]
        mixed = _sigmoid(ga_ref[r, :]) * branch[s][0] + _sigmoid(gb_ref[r, :]) * branch[s][1]
        outs.append(jnp.dot(mixed.astype(BF16), wo_ref[...], preferred_element_type=F32))
    logits = []
    for s in subs:
        r = rows_of[s]
        hn = _layer_norm(alpha * h_ref[r, :] + outs[s], g1_ref[...], b1_ref[...])
        hf_ref[r, :] = hn
        hp_ref[r, :] = _pack_pairs(hn)
        hn_hi = hn.astype(BF16)
        hn_lo = (hn - hn_hi.astype(F32)).astype(BF16)
        p_hi = jnp.dot(hn_hi, wr_ref[...], preferred_element_type=F32)
        p_lo = jnp.dot(hn_lo, wr_ref[:, :LANES], preferred_element_type=F32)
        logits.append(p_hi[:, :LANES] + p_hi[:, LANES:] + p_lo + br_ref[...])
    for s in subs:
        rows = rows_of[s]
        vals = logits[s]
        tops, sels = [], []
        idx_out = jnp.zeros((sub, LANES), I32)
        for j in range(TOP_K):
            m = jnp.max(vals, axis=-1, keepdims=True)
            idx = jnp.min(jnp.where(vals == m, lane, LANES), axis=-1, keepdims=True)
            sel = lane == idx
            tops.append(m)
            sels.append(sel)
            idx_out = jnp.where(lane == j, idx, idx_out)
            vals = jnp.where(sel, -jnp.inf, vals)
        es = [jnp.exp(m - tops[0]) for m in tops]
        inv = 1.0 / (es[0] + es[1] + es[2] + es[3])
        gate_out = jnp.zeros((sub, LANES), F32)
        onehot = jnp.zeros((sub, LANES), F32)
        for j in range(TOP_K):
            gate_out = jnp.where(lane == j, es[j] * inv, gate_out)
            onehot = onehot + sels[j].astype(F32)
        idx_ref[rows, :] = idx_out
        gate_ref[rows, :] = gate_out
        counts = counts + jnp.sum(onehot, axis=0, keepdims=True)

    @pl.when(i == 0)
    def _():
        cnt_ref[...] = jnp.zeros(cnt_ref.shape, F32)

    cnt_ref[0:1, :] = cnt_ref[0:1, :] + counts


def _merge(ya, yb, proj, h, wa, wb, wo, g1, b1, wr, br, alpha, tm, gacol, gbcol):
    R, D = h.shape
    C = ya.shape[1]
    const2 = lambda i: (0, 0)
    single = pl.Buffered(1)
    wr_hi = wr.astype(BF16)
    wr_split = jnp.concatenate([wr_hi, (wr - wr_hi.astype(F32)).astype(BF16)], axis=1)
    return pl.pallas_call(
        functools.partial(_merge_kernel, alpha=alpha),
        out_shape=(
            jax.ShapeDtypeStruct((R, D), F32),
            jax.ShapeDtypeStruct((R, D // 2), U32),
            jax.ShapeDtypeStruct((R, LANES), I32),
            jax.ShapeDtypeStruct((R, LANES), F32),
            jax.ShapeDtypeStruct((SUBLANES, LANES), F32),
        ),
        grid=(R // tm,),
        in_specs=[
            pl.BlockSpec((tm, C), lambda i: (i, 0)),
            pl.BlockSpec((tm, C), lambda i: (i, 0)),
            pl.BlockSpec((tm, D), lambda i: (i, gacol)),
            pl.BlockSpec((tm, D), lambda i: (i, gbcol)),
            pl.BlockSpec((tm, D), lambda i: (i, 0)),
            pl.BlockSpec((C, D), const2, pipeline_mode=single),
            pl.BlockSpec((C, D), const2, pipeline_mode=single),
            pl.BlockSpec((D, D), const2, pipeline_mode=single),
            pl.BlockSpec((1, D), const2),
            pl.BlockSpec((1, D), const2),
            pl.BlockSpec((D, 2 * LANES), const2, pipeline_mode=single),
            pl.BlockSpec((1, LANES), const2),
        ],
        out_specs=(
            pl.BlockSpec((tm, D), lambda i: (i, 0)),
            pl.BlockSpec((tm, D // 2), lambda i: (i, 0)),
            pl.BlockSpec((tm, LANES), lambda i: (i, 0)),
            pl.BlockSpec((tm, LANES), lambda i: (i, 0)),
            pl.BlockSpec((SUBLANES, LANES), const2),
        ),
        compiler_params=_params(("arbitrary",), 56),
        name="merge",
    )(ya, yb, proj, proj, h, wa, wb, wo, g1.reshape(1, D), b1.reshape(1, D), wr_split, br)


def _dest_kernel(idx_ref, start_ref, dest_ref, carry):
    i = pl.program_id(0)
    tr = idx_ref.shape[0]

    @pl.when(i == 0)
    def _():
        carry[...] = jnp.zeros(carry.shape, F32)

    idx = idx_ref[...]
    lane = lax.broadcasted_iota(I32, (tr, LANES), 1)
    ohs = [(lane == idx[:, j:j + 1]).astype(F32) for j in range(TOP_K)]
    oh = ohs[0] + ohs[1] + ohs[2] + ohs[3]
    ri = lax.broadcasted_iota(I32, (tr, tr), 0)
    ci = lax.broadcasted_iota(I32, (tr, tr), 1)
    stril = (ri > ci).astype(BF16)
    before = jnp.dot(stril, oh.astype(BF16), preferred_element_type=F32) + carry[0:1, :] + start_ref[...]
    dest = jnp.zeros((tr, LANES), I32)
    for j in range(TOP_K):
        pos = jnp.sum(ohs[j] * before, axis=-1, keepdims=True)
        dest = jnp.where(lane == j, pos.astype(I32), dest)
    dest_ref[...] = dest
    carry[0:1, :] = carry[0:1, :] + jnp.sum(oh, axis=0, keepdims=True)


def _dest(idx, starts, tr):
    R = idx.shape[0]
    return pl.pallas_call(
        _dest_kernel,
        out_shape=jax.ShapeDtypeStruct((R, LANES), I32),
        grid=(R // tr,),
        in_specs=[pl.BlockSpec((tr, LANES), lambda i: (i, 0)), pl.BlockSpec((1, LANES), lambda i: (0, 0))],
        out_specs=pl.BlockSpec((tr, LANES), lambda i: (i, 0)),
        scratch_shapes=[pltpu.VMEM((SUBLANES, LANES), F32)],
        compiler_params=_params(("arbitrary",), 32),
        name="moe_dest",
    )(idx, starts)


ROW_UNROLL = 4


def _dispatch_kernel(nv_ref, dest_ref, x_ref, xb_ref, zbuf, sem, zsem):
    tm = x_ref.shape[0]
    bm = zbuf.shape[0]
    n_blocks = xb_ref.shape[0] // bm

    @pl.when(pl.program_id(0) == 0)
    def _():
        zbuf[...] = jnp.zeros(zbuf.shape, U32)

        def zero_copy(b):
            return pltpu.make_async_copy(zbuf, xb_ref.at[pl.ds(pl.multiple_of(b * bm, bm), bm), :], zsem)

        def start(b, c):
            @pl.when(nv_ref[b] < bm)
            def _():
                zero_copy(b).start()
            return c

        def wait(b, c):
            @pl.when(nv_ref[b] < bm)
            def _():
                zero_copy(b).wait()
            return c

        lax.fori_loop(0, n_blocks, start, 0)
        lax.fori_loop(0, n_blocks, wait, 0)

    def row(r, c):
        for j in range(TOP_K):
            d = dest_ref[r * TOP_K + j]
            pltpu.make_async_copy(x_ref.at[pl.ds(r, 1), :], xb_ref.at[pl.ds(d, 1), :], sem).start(priority=j % 2)
        return c

    lax.fori_loop(0, tm, row, 0, unroll=ROW_UNROLL)
    for j in range(TOP_K):
        pltpu.make_async_copy(x_ref, xb_ref.at[pl.ds(0, tm), :], sem).wait()


def _dispatch(n_valid, dest_flat, hp, m_pad, tm, bm):
    R, W = hp.shape
    grid_spec = pltpu.PrefetchScalarGridSpec(
        num_scalar_prefetch=1,
        grid=(R // tm,),
        in_specs=[
            pl.BlockSpec((tm * TOP_K,), lambda i, nv: (i,), memory_space=pltpu.SMEM),
            pl.BlockSpec((tm, W), lambda i, nv: (i, 0)),
        ],
        out_specs=pl.BlockSpec(memory_space=pl.ANY),
        scratch_shapes=[pltpu.VMEM((bm, W), U32), pltpu.SemaphoreType.DMA, pltpu.SemaphoreType.DMA],
    )
    return pl.pallas_call(
        _dispatch_kernel,
        out_shape=jax.ShapeDtypeStruct((m_pad, W), U32),
        grid_spec=grid_spec,
        compiler_params=_params(("arbitrary",), 32),
        name="moe_dispatch",
    )(n_valid, dest_flat, hp)


def _deint_kernel(w_ref, p_ref, g_ref, l_ref):
    w = w_ref[0].astype(BF16)
    perm = p_ref[...]
    width = perm.shape[0]
    half = width // 2
    for g in range(w.shape[1] // width):
        t = jnp.dot(w[:, g * width:(g + 1) * width], perm, preferred_element_type=F32)
        g_ref[0, :, g * half:(g + 1) * half] = t[:, :half].astype(BF16)
        l_ref[0, :, g * half:(g + 1) * half] = t[:, half:].astype(BF16)


def _deinterleave(w_gu, tr):
    N, D, DE2 = w_gu.shape
    DE = DE2 // 2
    width = 2 * LANES
    src = lax.broadcasted_iota(I32, (width, width), 0)
    dst = lax.broadcasted_iota(I32, (width, width), 1)
    perm = (src == jnp.where(dst < LANES, 2 * dst, 2 * (dst - LANES) + 1)).astype(BF16)
    out = jax.ShapeDtypeStruct((N, D, DE), BF16)
    return pl.pallas_call(
        _deint_kernel,
        out_shape=(out, out),
        grid=(N, D // tr),
        in_specs=[
            pl.BlockSpec((1, tr, DE2), lambda n, i: (n, i, 0)),
            pl.BlockSpec((width, width), lambda n, i: (0, 0)),
        ],
        out_specs=(pl.BlockSpec((1, tr, DE), lambda n, i: (n, i, 0)),
                   pl.BlockSpec((1, tr, DE), lambda n, i: (n, i, 0))),
        compiler_params=_params(("arbitrary", "arbitrary"), 40),
        name="deinterleave",
    )(w_gu, perm)


FFN_GATES = 4


def _ffn_kernel(be_ref, nv_ref, nu_ref, x_ref, wg_ref, bg_ref, wl_ref, bl_ref, wd_ref, bd_ref, o_ref):
    del be_ref
    b = pl.program_id(0)
    tm = x_ref.shape[0]
    q = tm // FFN_GATES
    used = b < nu_ref[0]
    gates_needed = jnp.maximum((nv_ref[b] + q - 1) // q, 1)

    def ffn_rows(rows):
        x = _unpack_pairs(x_ref[rows, :])
        glu = jnp.dot(x, wg_ref[0], preferred_element_type=F32) + bg_ref[0]
        lin = jnp.dot(x, wl_ref[0], preferred_element_type=F32) + bl_ref[0]
        glu = jnp.minimum(glu, SWIGLU_LIMIT)
        lin = jnp.clip(lin, -SWIGLU_LIMIT, SWIGLU_LIMIT)
        act = glu * _sigmoid(SWIGLU_ALPHA * glu) * (lin + 1.0)
        y = jnp.dot(act.astype(BF16), wd_ref[0], preferred_element_type=F32) + bd_ref[0]
        o_ref[rows, :] = _pack_pairs(y)

    @pl.when(jnp.logical_not(used))
    def _():
        o_ref[...] = jnp.zeros(o_ref.shape, U32)

    for k in range(1, FFN_GATES + 1):
        @pl.when(jnp.logical_and(used, gates_needed == k))
        def _(k=k):
            ffn_rows(slice(0, k * q))
            if k < FFN_GATES:
                o_ref[k * q:, :] = jnp.zeros((tm - k * q, o_ref.shape[1]), U32)


def _ffn(block_exp, n_valid, n_used, xb, wg, bg, wl, bl, wd, bd, tm):
    m_pad, W = xb.shape
    _, D, DE = wg.shape
    n_blocks = m_pad // tm

    def rows(b, be, nv, nu):
        return (jnp.minimum(b, nu[0] - 1), 0)

    def expert(b, be, nv, nu):
        return (be[jnp.minimum(b, nu[0] - 1)], 0, 0)

    grid_spec = pltpu.PrefetchScalarGridSpec(
        num_scalar_prefetch=3,
        grid=(n_blocks,),
        in_specs=[
            pl.BlockSpec((tm, W), rows),
            pl.BlockSpec((1, D, DE), expert),
            pl.BlockSpec((1, 1, DE), expert),
            pl.BlockSpec((1, D, DE), expert),
            pl.BlockSpec((1, 1, DE), expert),
            pl.BlockSpec((1, DE, D), expert),
            pl.BlockSpec((1, 1, D), expert),
        ],
        out_specs=pl.BlockSpec((tm, W), lambda b, be, nv, nu: (b, 0)),
    )
    return pl.pallas_call(
        _ffn_kernel,
        out_shape=jax.ShapeDtypeStruct((m_pad, W), U32),
        grid_spec=grid_spec,
        compiler_params=_params(("arbitrary",), 56),
        name="moe_ffn",
    )(block_exp, n_valid, n_used, xb, wg, bg, wl, bl, wd, bd)


def _combine_kernel(dest_ref, dnext_ref, gate_ref, h_ref, g2_ref, b2_ref, yb_ref, of_ref, ob_ref, buf, sems,
                    *, alpha):
    i = pl.program_id(0)
    tm, D = h_ref.shape
    slot = i % 2

    def gather(d_ref, s):
        def row(r, c):
            for j in range(TOP_K):
                d = d_ref[r * TOP_K + j]
                pltpu.make_async_copy(
                    yb_ref.at[pl.ds(d, 1), :], buf.at[s, j, pl.ds(r, 1), :], sems.at[s]).start(priority=j % 2)
            return c

        lax.fori_loop(0, tm, row, 0, unroll=ROW_UNROLL)

    @pl.when(i == 0)
    def _():
        gather(dest_ref, slot)

    @pl.when(i + 1 < pl.num_programs(0))
    def _():
        gather(dnext_ref, 1 - slot)

    for j in range(TOP_K):
        pltpu.make_async_copy(yb_ref.at[pl.ds(0, tm), :], buf.at[slot, j], sems.at[slot]).wait()

    gate = gate_ref[...]
    moe = None
    for j in range(TOP_K):
        term = gate[:, j:j + 1] * _unpack_pairs_f32(buf[slot, j])
        moe = term if moe is None else moe + term
    y = _layer_norm(alpha * h_ref[...] + moe, g2_ref[...], b2_ref[...])
    of_ref[...] = y
    ob_ref[...] = y.astype(BF16)


def _combine(dest_flat, gate, h, g2, b2, yb, alpha, tm):
    R, D = h.shape
    W = yb.shape[1]
    n = R // tm
    row = pl.BlockSpec((tm, D), lambda i: (i, 0))
    vec = pl.BlockSpec((1, D), lambda i: (0, 0))
    return pl.pallas_call(
        functools.partial(_combine_kernel, alpha=alpha),
        out_shape=(jax.ShapeDtypeStruct((R, D), F32), jax.ShapeDtypeStruct((R, D), BF16)),
        grid=(n,),
        in_specs=[
            pl.BlockSpec((tm * TOP_K,), lambda i: (i,), memory_space=pltpu.SMEM),
            pl.BlockSpec((tm * TOP_K,), lambda i: (jnp.minimum(i + 1, n - 1),), memory_space=pltpu.SMEM),
            pl.BlockSpec((tm, LANES), lambda i: (i, 0)),
            row, vec, vec,
            pl.BlockSpec(memory_space=pl.ANY),
        ],
        out_specs=(row, row),
        scratch_shapes=[pltpu.VMEM((2, TOP_K, tm, W), U32), pltpu.SemaphoreType.DMA((2,))],
        compiler_params=_params(("arbitrary",), 48),
        name="moe_combine",
    )(dest_flat, dest_flat, gate, h, g2.reshape(1, D), b2.reshape(1, D), yb)


def _lane_pad(v, offset):
    return jnp.zeros((1, LANES), F32).at[0, offset:offset + v.shape[0]].set(v.astype(F32))


def kernel(x, meta_tokens, ln_in_g, ln_in_b, w_in, rg_conv_w, rg_conv_b, rg_wx, rg_bx, rg_wa, rg_ba, rg_lambda, gdn_conv_w, gdn_a_log, gdn_dt_bias, gdn_norm_w, w_branch_a, w_branch_b, w_out, ln1_g, ln1_b, w_router, b_router, w_gate_up, b_gate_up, w_down, b_down, ln2_g, ln2_b):
    B, seq, D = x.shape
    depth = w_in.shape[0]
    E = w_router.shape[2]
    DE = w_down.shape[2]
    d_rnn = D // 2
    assert D == 2 * GDN_V_DIM and d_rnn == GDN_V_DIM and E <= LANES
    alpha = float((2 * depth) ** 0.25)

    T = N_META + seq
    pad = (-T) % CHUNK
    Tp = T + pad
    R = B * Tp

    tm_ln = _pick_tile(R, 512)
    tm_in = _pick_tile(R, 1040)
    tn_in = 1024
    tt_rg = _pick_tile(Tp, 416, SUBLANES)
    tt_gdn = _pick_tile(Tp, 320, CHUNK)
    tm_mg = _pick_tile(R, 256)
    tm_moe = _pick_tile(R, 256)
    bm_ffn = 2 * tm_moe
    n_blocks = -(-(R * TOP_K) // bm_ffn) + E
    m_pad = n_blocks * bm_ffn

    meta = jnp.broadcast_to(meta_tokens[None].astype(x.dtype), (B, N_META, D))
    h0 = jnp.concatenate([jnp.zeros((B, pad, D), x.dtype), meta, x], axis=1).reshape(R, D)
    h, hb = _ln_in(h0, ln_in_g, ln_in_b, tm_ln)

    wg_all, wl_all = _deinterleave(w_gate_up.reshape(depth * E, D, 2 * DE), _pick_tile(D, 512))
    wd_all = w_down.reshape(depth * E, DE, D).astype(BF16)
    bg_all = b_gate_up.reshape(depth * E, 1, 2 * DE)[:, :, 0::2]
    bl_all = b_gate_up.reshape(depth * E, 1, 2 * DE)[:, :, 1::2]
    bd_all = b_down.reshape(depth * E, 1, D)

    o_rx = 0
    o_q = 2 * d_rnn
    o_z = o_q + 2 * GDN_QK_DIM + GDN_V_DIM
    o_beta = o_z + GDN_V_DIM
    o_ga = o_beta + 2 * GDN_H_V
    o_gb = o_ga + D
    block_start = jnp.arange(n_blocks, dtype=I32) * bm_ffn

    for l in range(depth):
        wl = w_in[l]
        w_main = jnp.concatenate(
            [wl[:, o_ga:o_ga + D], wl[:, o_gb:o_gb + D], wl[:, o_q:o_z], wl[:, o_rx:o_q], wl[:, o_z:o_beta]],
            axis=1).astype(BF16)
        w_small = jnp.zeros((D, LANES), F32).at[:, :2 * GDN_H_V].set(wl[:, o_beta:o_ga]).astype(BF16)
        proj, small = _inproj(hb, w_main, w_small, tm_in, tn_in)

        ya = _rglru(proj, rg_conv_w[l], rg_conv_b[l], rg_wx[l].astype(BF16), rg_bx[l], rg_wa[l].astype(BF16),
                    rg_ba[l], rg_lambda[l], B, Tp, pad, tt_rg, xcol=(2 * D + 2048) // d_rnn,
                    ycol=(2 * D + 2048) // d_rnn + 1)
        yb = _gdn(proj, small, gdn_conv_w[l], _lane_pad(gdn_a_log[l], GDN_H_V), _lane_pad(gdn_dt_bias[l], GDN_H_V),
                  gdn_norm_w[l].reshape(1, GDN_DV).astype(F32), B, Tp, pad, tt_gdn, qkvcol=2,
                  zcol=(2 * D + 2048 + 2 * d_rnn) // GDN_V_DIM)

        wr = jnp.zeros((D, LANES), F32).at[:, :E].set(w_router[l])
        br = jnp.full((1, LANES), -jnp.inf, F32).at[0, :E].set(b_router[l])
        hn, hp, idx, gate, counts = _merge(
            ya, yb, proj, h, w_branch_a[l].astype(BF16), w_branch_b[l].astype(BF16), w_out[l].astype(BF16),
            ln1_g[l], ln1_b[l], wr, br, alpha, tm_mg, gacol=0, gbcol=1)

        cnt = counts[0, :E].astype(I32)
        padded = (cnt + bm_ffn - 1) // bm_ffn * bm_ffn
        pad_ends = jnp.cumsum(padded)
        starts = pad_ends - padded
        block_exp = jnp.minimum(jnp.sum((pad_ends[None, :] <= block_start[:, None]).astype(I32), axis=1), E - 1)
        onehot = (block_exp[:, None] == jnp.arange(E, dtype=I32)[None, :]).astype(I32)
        n_valid = jnp.clip(jnp.sum(onehot * (cnt + starts)[None, :], axis=1) - block_start, 0, bm_ffn).astype(I32)
        n_used = (pad_ends[E - 1:E] // bm_ffn).astype(I32)

        dest = _dest(idx, _lane_pad(starts, 0), tm_moe)
        dest_flat = dest[:, :TOP_K].reshape(R * TOP_K)

        xb = _dispatch(n_valid, dest_flat, hp, m_pad, tm_moe, bm_ffn)
        ybk = _ffn(block_exp + l * E, n_valid, n_used, xb, wg_all, bg_all, wl_all, bl_all, wd_all, bd_all, bm_ffn)
        h, hb = _combine(dest_flat, gate, hn, ln2_g[l], ln2_b[l], ybk, alpha, tm_moe)

    return h.reshape(B, Tp, D)[:, pad + N_META:]
```

```python
import functools

import jax
import jax.numpy as jnp
from jax import lax
from jax.experimental import pallas as pl
from jax.experimental.pallas import tpu as pltpu

F32 = jnp.float32
BF16 = jnp.bfloat16
I32 = jnp.int32
U32 = jnp.uint32

N_META = 16
CONV_W = 4
CHUNK = 64
TOP_K = 4
RG_BLOCKS = 8
RG_C = 8.0
GDN_DK = 128
GDN_DV = 128
GDN_H_QK = 4
GDN_H_V = 8
GDN_QK_DIM = GDN_H_QK * GDN_DK
GDN_V_DIM = GDN_H_V * GDN_DV
SWIGLU_LIMIT = 7.0
SWIGLU_ALPHA = 1.702
LN_EPS = 1e-5
RMS_EPS = 1e-6

LANES = 128
SUBLANES = 8
MIB = 1024 * 1024
HIGHEST = lax.Precision.HIGHEST


def _pick_tile(n, target, mult=16):
    best = None
    for t in range(mult, min(n, target) + 1, mult):
        if n % t == 0:
            best = t
    assert best is not None, (n, target, mult)
    return best


def _params(sem, vmem_mib):
    return pltpu.CompilerParams(dimension_semantics=sem, vmem_limit_bytes=vmem_mib * MIB)


def _layer_norm(x, g, b):
    mu = jnp.mean(x, axis=-1, keepdims=True)
    xc = x - mu
    var = jnp.mean(xc * xc, axis=-1, keepdims=True)
    return xc * lax.rsqrt(var + LN_EPS) * g + b


def _sigmoid(x):
    return 1.0 / (1.0 + jnp.exp(-x))


def _softplus(x):
    return jnp.maximum(x, 0.0) + jnp.log1p(jnp.exp(-jnp.abs(x)))


def _silu(x):
    return x * _sigmoid(x)


def _gelu_tanh(x):
    return 0.5 * x * (1.0 + jnp.tanh(0.7978845608028654 * (x + 0.044715 * x * x * x)))


def _pack_pairs(x):
    n = x.shape[1] // 2
    xb = x.astype(BF16).astype(F32)
    lo = lax.bitcast_convert_type(xb[:, :n], U32) >> 16
    hi = lax.bitcast_convert_type(xb[:, n:], U32) & jnp.uint32(0xFFFF0000)
    return lo | hi


def _unpack_pairs_f32(w):
    lo = lax.bitcast_convert_type(w << 16, F32)
    hi = lax.bitcast_convert_type(w & jnp.uint32(0xFFFF0000), F32)
    return jnp.concatenate([lo, hi], axis=1)


def _unpack_pairs(w):
    return _unpack_pairs_f32(w).astype(BF16)


def _ln_in_kernel(x_ref, meta_ref, g_ref, b_ref, of_ref, ob_ref, xin, sem, *, pad):
    bi = pl.program_id(0)
    t = pl.program_id(1)
    tt = xin.shape[0]
    lead = pad + meta_ref.shape[0]

    @pl.when(t == 0)
    def _():
        if pad:
            xin[0:pad, :] = jnp.zeros((pad, xin.shape[1]), F32)
        xin[pad:lead, :] = meta_ref[...]
        if tt > lead:
            cp = pltpu.make_async_copy(
                x_ref.at[bi, pl.ds(0, tt - lead), :], xin.at[pl.ds(lead, tt - lead), :], sem)
            cp.start()
            cp.wait()

    @pl.when(t > 0)
    def _():
        start = pl.multiple_of(t * tt - lead, SUBLANES)
        cp = pltpu.make_async_copy(x_ref.at[bi, pl.ds(start, tt), :], xin, sem)
        cp.start()
        cp.wait()

    y = _layer_norm(xin[...], g_ref[...], b_ref[...])
    of_ref[...] = y
    ob_ref[...] = y.astype(BF16)


def _ln_in(x, meta, g, b, pad, tt):
    B, seq, D = x.shape
    lead = pad + meta.shape[0]
    Tp = seq + lead
    nT = Tp // tt
    assert lead % SUBLANES == 0 and lead <= tt and Tp % tt == 0
    row = pl.BlockSpec((tt, D), lambda bi, t: (bi * nT + t, 0))
    vec = pl.BlockSpec((1, D), lambda bi, t: (0, 0))
    return pl.pallas_call(
        functools.partial(_ln_in_kernel, pad=pad),
        out_shape=(jax.ShapeDtypeStruct((B * Tp, D), F32), jax.ShapeDtypeStruct((B * Tp, D), BF16)),
        grid=(B, nT),
        in_specs=[pl.BlockSpec(memory_space=pl.ANY), pl.BlockSpec(meta.shape, lambda bi, t: (0, 0)), vec, vec],
        out_specs=(row, row),
        scratch_shapes=[pltpu.VMEM((tt, D), F32), pltpu.SemaphoreType.DMA],
        compiler_params=_params(("arbitrary", "arbitrary"), 48),
        name="ln_in",
    )(x, meta.astype(F32), g.reshape(1, D), b.reshape(1, D))


def _inproj_kernel(x_ref, w_ref, ws_ref, o_ref, os_ref):
    x = x_ref[...]
    o_ref[...] = jnp.dot(x, w_ref[...], preferred_element_type=F32)

    @pl.when(pl.program_id(1) == 0)
    def _():
        os_ref[...] = jnp.dot(x, ws_ref[...], preferred_element_type=F32)


def _inproj(hb, w_main, w_small, tm, tn):
    R, D = hb.shape
    N = w_main.shape[1]
    return pl.pallas_call(
        _inproj_kernel,
        out_shape=(jax.ShapeDtypeStruct((R, N), F32), jax.ShapeDtypeStruct((R, LANES), F32)),
        grid=(R // tm, N // tn),
        in_specs=[
            pl.BlockSpec((tm, D), lambda i, j: (i, 0)),
            pl.BlockSpec((D, tn), lambda i, j: (0, j)),
            pl.BlockSpec((D, LANES), lambda i, j: (0, 0)),
        ],
        out_specs=(
            pl.BlockSpec((tm, tn), lambda i, j: (i, j)),
            pl.BlockSpec((tm, LANES), lambda i, j: (i, 0)),
        ),
        compiler_params=_params(("arbitrary", "arbitrary"), 48),
        name="inproj",
    )(hb, w_main, w_small)


def _causal_conv(x, xbuf, cw_ref, tt):
    xbuf[SUBLANES:SUBLANES + tt, :] = x
    y = cw_ref[CONV_W - 1:CONV_W, :] * x
    for j in range(CONV_W - 1):
        off = SUBLANES - (CONV_W - 1) + j
        y = y + cw_ref[j:j + 1, :] * xbuf[off:off + tt, :]
    xbuf[0:SUBLANES, :] = xbuf[tt:tt + SUBLANES, :]
    return y


def _rglru_kernel(x_ref, y_ref, cw_ref, cb_ref, wx_ref, bx_ref, wa_ref, ba_ref, lam_ref,
                  o_ref, xbuf, a_s, u_s, hc, *, pad):
    t = pl.program_id(1)
    tt, C = x_ref.shape
    bs = C // RG_BLOCKS

    @pl.when(t == 0)
    def _():
        xbuf[0:SUBLANES, :] = jnp.zeros((SUBLANES, C), F32)
        hc[...] = jnp.zeros((SUBLANES, C), F32)

    row = lax.broadcasted_iota(I32, (tt, 1), 0)
    valid = (row + t * tt) >= pad
    x = jnp.where(valid, x_ref[...], 0.0)
    xc = _causal_conv(x, xbuf, cw_ref, tt) + cb_ref[...]

    xcb = xc.astype(BF16)
    gi = jnp.concatenate(
        [jnp.dot(xcb[:, h * bs:(h + 1) * bs], wx_ref[h], preferred_element_type=F32) for h in range(RG_BLOCKS)],
        axis=1)
    gr = jnp.concatenate(
        [jnp.dot(xcb[:, h * bs:(h + 1) * bs], wa_ref[h], preferred_element_type=F32) for h in range(RG_BLOCKS)],
        axis=1)
    gi = _sigmoid(gi + bx_ref[...])
    gr = _sigmoid(gr + ba_ref[...])
    log_a = (-RG_C) * gr * _softplus(-lam_ref[...])
    a = jnp.exp(log_a)
    u = jnp.sqrt(-jnp.tanh(log_a) * (a * a + 1.0)) * gi * xc
    u = jnp.where(valid, u, 0.0)

    rowmod = row % SUBLANES
    for d in (1, 2, 4):
        m = rowmod >= d
        a_sh = pltpu.roll(a, d, 0)
        u_sh = pltpu.roll(u, d, 0)
        u = jnp.where(m, a * u_sh + u, u)
        a = jnp.where(m, a * a_sh, a)
    a_s[...] = a
    u_s[...] = u

    def group(g, hprev):
        r0 = pl.multiple_of(g * SUBLANES, SUBLANES)
        h = a_s[pl.ds(r0, SUBLANES), :] * hprev + u_s[pl.ds(r0, SUBLANES), :]
        u_s[pl.ds(r0, SUBLANES), :] = h
        return jnp.broadcast_to(h[SUBLANES - 1:SUBLANES, :], (SUBLANES, C))

    hc[...] = lax.fori_loop(0, tt // SUBLANES, group, hc[...])
    o_ref[...] = (u_s[...] * _gelu_tanh(y_ref[...])).astype(BF16)


def _rglru(proj, cw, cb, wx, bx, wa, ba, lam, B, Tp, pad, tt, xcol, ycol):
    R = proj.shape[0]
    C = cw.shape[1]
    nT = Tp // tt
    vec = pl.BlockSpec((1, C), lambda b, t: (0, 0))
    wspec = pl.BlockSpec(wx.shape, lambda b, t: (0, 0, 0))
    return pl.pallas_call(
        functools.partial(_rglru_kernel, pad=pad),
        out_shape=jax.ShapeDtypeStruct((R, C), BF16),
        grid=(B, nT),
        in_specs=[
            pl.BlockSpec((tt, C), lambda b, t: (b * nT + t, xcol)),
            pl.BlockSpec((tt, C), lambda b, t: (b * nT + t, ycol)),
            pl.BlockSpec((CONV_W, C), lambda b, t: (0, 0)),
            vec, wspec, vec, wspec, vec, vec,
        ],
        out_specs=pl.BlockSpec((tt, C), lambda b, t: (b * nT + t, 0)),
        scratch_shapes=[
            pltpu.VMEM((tt + SUBLANES, C), F32),
            pltpu.VMEM((tt, C), F32),
            pltpu.VMEM((tt, C), F32),
            pltpu.VMEM((SUBLANES, C), F32),
        ],
        compiler_params=_params(("arbitrary", "arbitrary"), 48),
        name="rglru",
    )(proj, proj, cw, cb.reshape(1, C), wx, bx.reshape(1, C), wa, ba.reshape(1, C), lam.reshape(1, C))


def _bdot(a, b):
    return jnp.dot(a.astype(BF16), b.astype(BF16), preferred_element_type=F32)


def _bdot_nt(a, b):
    return lax.dot_general(a.astype(BF16), b.astype(BF16), (((1,), (1,)), ((), ())), preferred_element_type=F32)


def _bdot_tn(a, b):
    return lax.dot_general(a.astype(BF16), b.astype(BF16), (((0,), (0,)), ((), ())), preferred_element_type=F32)


def _unit_lower_inverses(Ls, masks):
    eye = masks[0]
    dinvs = [eye - L * masks[1] for L in Ls]
    for m in masks[2:]:
        ts = [_bdot(d, L * m) for d, L in zip(dinvs, Ls)]
        dinvs = [d - _bdot(t, d) for d, t in zip(dinvs, ts)]
    return dinvs


def _gdn_kernel(qkv_ref, z_ref, sm_ref, cw_ref, alog_ref, dtb_ref, nw_ref,
                o_ref, xbuf, q_s, k_s, v_s, b_s, g_s, S_ref, val_s, kq_s, at_s, kd_s, eg_s, *, pad):
    t = pl.program_id(1)
    tt, CQ = qkv_ref.shape
    ncht = tt // CHUNK
    rep = GDN_H_V // GDN_H_QK

    @pl.when(t == 0)
    def _():
        xbuf[0:SUBLANES, :] = jnp.zeros((SUBLANES, CQ), F32)
        S_ref[...] = jnp.zeros(S_ref.shape, F32)

    row = lax.broadcasted_iota(I32, (tt, 1), 0)
    valid = (row + t * tt) >= pad
    x = jnp.where(valid, qkv_ref[...], 0.0)
    xc = _silu(_causal_conv(x, xbuf, cw_ref, tt))

    for hq in range(GDN_H_QK):
        qh = xc[:, hq * GDN_DK:(hq + 1) * GDN_DK]
        kh = xc[:, GDN_QK_DIM + hq * GDN_DK:GDN_QK_DIM + (hq + 1) * GDN_DK]
        qn = qh * lax.rsqrt(jnp.sum(qh * qh, axis=-1, keepdims=True) + RMS_EPS) * (GDN_DK ** -0.5)
        kn = kh * lax.rsqrt(jnp.sum(kh * kh, axis=-1, keepdims=True) + RMS_EPS)
        q_s[:, hq * GDN_DK:(hq + 1) * GDN_DK] = qn
        k_s[:, hq * GDN_DK:(hq + 1) * GDN_DK] = kn
    v_s[...] = xc[:, 2 * GDN_QK_DIM:]

    sm = sm_ref[...]
    b_s[...] = _sigmoid(sm)
    g = -jnp.exp(alog_ref[...]) * _softplus(sm + dtb_ref[...])
    g_s[...] = jnp.where(valid, g, 0.0)

    ri = lax.broadcasted_iota(I32, (CHUNK, CHUNK), 0)
    ci = lax.broadcasted_iota(I32, (CHUNK, CHUNK), 1)
    causal = ri >= ci
    strict = ri > ci
    tril = causal.astype(F32)
    masks = [(ri == ci).astype(F32)]
    s = 1
    while s < CHUNK:
        masks.append(((ri // (2 * s) == ci // (2 * s)) & (ri % (2 * s) >= s) & (ci % (2 * s) < s)).astype(F32))
        s *= 2
    nw = nw_ref[...]

    heads = range(GDN_H_V)
    chunks = range(ncht)
    inst = [(c, h) for c in chunks for h in heads]

    rows = [slice(c * CHUNK, (c + 1) * CHUNK) for c in chunks]
    G = [jnp.dot(tril, g_s[rows[c], :], precision=HIGHEST, preferred_element_type=F32) for c in chunks]
    GT = [G[c].T for c in chunks]
    bet = [b_s[rows[c], :] for c in chunks]
    qs = [[q_s[rows[c], hq * GDN_DK:(hq + 1) * GDN_DK] for hq in range(GDN_H_QK)] for c in chunks]
    ks_ = [[k_s[rows[c], hq * GDN_DK:(hq + 1) * GDN_DK] for hq in range(GDN_H_QK)] for c in chunks]
    kk = [[_bdot_nt(k, k) for k in ks_[c]] for c in chunks]
    qk = [[_bdot_nt(q, k) for q, k in zip(qs[c], ks_[c])] for c in chunks]
    bcol = {(c, h): bet[c][:, h:h + 1] for c, h in inst}
    Gc = {(c, h): G[c][:, GDN_H_V + h:GDN_H_V + h + 1] for c, h in inst}
    decay = {(c, h): jnp.where(causal, jnp.exp(jnp.minimum(
        Gc[c, h] - GT[c][GDN_H_V + h:GDN_H_V + h + 1, :], 0.0)), 0.0) for c, h in inst}
    eG = {i: jnp.exp(Gc[i]) for i in inst}
    g_last = {i: Gc[i][CHUNK - 1:CHUNK, :] for i in inst}
    Ls = [jnp.where(strict, kk[c][h // rep] * bcol[c, h] * decay[c, h], 0.0) for c, h in inst]
    Ts = dict(zip(inst, _unit_lower_inverses(Ls, masks)))
    for c, h in inst:
        k = ks_[c][h // rep]
        sol = _bdot(Ts[c, h], jnp.concatenate(
            [v_s[rows[c], h * GDN_DV:(h + 1) * GDN_DV] * bcol[c, h], k * (bcol[c, h] * eG[c, h])], axis=1))
        val_s[c, h] = sol[:, :GDN_DV]
        kq_s[c, h] = jnp.concatenate([sol[:, GDN_DV:], qs[c][h // rep] * eG[c, h]], axis=0).astype(BF16)
        at_s[c, h] = (qk[c][h // rep] * decay[c, h]).astype(BF16)
        kd_s[c, h] = (k * jnp.exp(g_last[c, h] - Gc[c, h])).astype(BF16)
        eg_s[c, h] = jnp.broadcast_to(jnp.exp(g_last[c, h]), (SUBLANES, GDN_DV))

    for c in chunks:
        zc = z_ref[rows[c], :]
        Sold = [S_ref[h] for h in heads]
        PS = [_bdot(kq_s[c, h], Sold[h]) for h in heads]
        v_new = [val_s[c, h] - PS[h][:CHUNK] for h in heads]
        o = [PS[h][CHUNK:] + _bdot(at_s[c, h], v_new[h]) for h in heads]
        for h in heads:
            S_ref[h] = Sold[h] * eg_s[c, h][0:1, :] + _bdot_tn(kd_s[c, h], v_new[h])
        for h in heads:
            oh = o[h] * lax.rsqrt(jnp.mean(o[h] * o[h], axis=-1, keepdims=True) + RMS_EPS) * nw
            oh = oh * _silu(zc[:, h * GDN_DV:(h + 1) * GDN_DV])
            o_ref[rows[c], h * GDN_DV:(h + 1) * GDN_DV] = oh.astype(BF16)


def _gdn(proj, small, cw, alog_pad, dtb_pad, nw, B, Tp, pad, tt, qkvcol, zcol):
    R = proj.shape[0]
    CQ = cw.shape[1]
    nT = Tp // tt
    lane_vec = pl.BlockSpec((1, LANES), lambda b, t: (0, 0))
    return pl.pallas_call(
        functools.partial(_gdn_kernel, pad=pad),
        out_shape=jax.ShapeDtypeStruct((R, GDN_V_DIM), BF16),
        grid=(B, nT),
        in_specs=[
            pl.BlockSpec((tt, CQ), lambda b, t: (b * nT + t, qkvcol)),
            pl.BlockSpec((tt, GDN_V_DIM), lambda b, t: (b * nT + t, zcol)),
            pl.BlockSpec((tt, LANES), lambda b, t: (b * nT + t, 0)),
            pl.BlockSpec((CONV_W, CQ), lambda b, t: (0, 0)),
            lane_vec, lane_vec, lane_vec,
        ],
        out_specs=pl.BlockSpec((tt, GDN_V_DIM), lambda b, t: (b * nT + t, 0)),
        scratch_shapes=[
            pltpu.VMEM((tt + SUBLANES, CQ), F32),
            pltpu.VMEM((tt, GDN_QK_DIM), F32),
            pltpu.VMEM((tt, GDN_QK_DIM), F32),
            pltpu.VMEM((tt, GDN_V_DIM), F32),
            pltpu.VMEM((tt, LANES), F32),
            pltpu.VMEM((tt, LANES), F32),
            pltpu.VMEM((GDN_H_V, GDN_DK, GDN_DV), F32),
            pltpu.VMEM((tt // CHUNK, GDN_H_V, CHUNK, GDN_DV), F32),
            pltpu.VMEM((tt // CHUNK, GDN_H_V, 2 * CHUNK, GDN_DK), BF16),
            pltpu.VMEM((tt // CHUNK, GDN_H_V, CHUNK, CHUNK), BF16),
            pltpu.VMEM((tt // CHUNK, GDN_H_V, CHUNK, GDN_DK), BF16),
            pltpu.VMEM((tt // CHUNK, GDN_H_V, SUBLANES, GDN_DV), F32),
        ],
        compiler_params=_params(("arbitrary", "arbitrary"), 48),
        name="gdn",
    )(proj, proj, small, cw, alog_pad, dtb_pad, nw)


MERGE_SUBTILES = 2


def _merge_kernel(ya_ref, yb_ref, ga_ref, gb_ref, h_ref, wa_ref, wb_ref, wo_ref, g1_ref, b1_ref,
                  wr_ref, br_ref, hf_ref, hp_ref, idx_ref, gate_ref, cnt_ref, *, alpha):
    i = pl.program_id(0)
    tm, D = h_ref.shape
    sub = tm // MERGE_SUBTILES
    lane = lax.broadcasted_iota(I32, (sub, LANES), 1)
    counts = jnp.zeros((1, LANES), F32)
    subs = range(MERGE_SUBTILES)
    rows_of = [slice(s * sub, (s + 1) * sub) for s in subs]
    branch = [(jnp.dot(ya_ref[r, :], wa_ref[...], preferred_element_type=F32),
               jnp.dot(yb_ref[r, :], wb_ref[...], preferred_element_type=F32)) for r in rows_of]
    outs = []
    for s in subs:
        r = rows_of[s]
        mixed = _sigmoid(ga_ref[r, :]) * branch[s][0] + _sigmoid(gb_ref[r, :]) * branch[s][1]
        outs.append(jnp.dot(mixed.astype(BF16), wo_ref[...], preferred_element_type=F32))
    logits = []
    for s in subs:
        r = rows_of[s]
        hn = _layer_norm(alpha * h_ref[r, :] + outs[s], g1_ref[...], b1_ref[...])
        hf_ref[r, :] = hn
        hp_ref[r, :] = _pack_pairs(hn)
        hn_hi = hn.astype(BF16)
        hn_lo = (hn - hn_hi.astype(F32)).astype(BF16)
        p_hi = jnp.dot(hn_hi, wr_ref[...], preferred_element_type=F32)
        p_lo = jnp.dot(hn_lo, wr_ref[:, :LANES], preferred_element_type=F32)
        logits.append(p_hi[:, :LANES] + p_hi[:, LANES:] + p_lo + br_ref[...])
    for s in subs:
        rows = rows_of[s]
        vals = logits[s]
        tops, sels = [], []
        idx_out = jnp.zeros((sub, LANES), I32)
        for j in range(TOP_K):
            m = jnp.max(vals, axis=-1, keepdims=True)
            idx = jnp.min(jnp.where(vals == m, lane, LANES), axis=-1, keepdims=True)
            sel = lane == idx
            tops.append(m)
            sels.append(sel)
            idx_out = jnp.where(lane == j, idx, idx_out)
            vals = jnp.where(sel, -jnp.inf, vals)
        es = [jnp.exp(m - tops[0]) for m in tops]
        inv = 1.0 / (es[0] + es[1] + es[2] + es[3])
        gate_out = jnp.zeros((sub, LANES), F32)
        onehot = jnp.zeros((sub, LANES), F32)
        for j in range(TOP_K):
            gate_out = jnp.where(lane == j, es[j] * inv, gate_out)
            onehot = onehot + sels[j].astype(F32)
        idx_ref[rows, :] = idx_out
        gate_ref[rows, :] = gate_out
        counts = counts + jnp.sum(onehot, axis=0, keepdims=True)

    @pl.when(i == 0)
    def _():
        cnt_ref[...] = jnp.zeros(cnt_ref.shape, F32)

    cnt_ref[0:1, :] = cnt_ref[0:1, :] + counts


def _merge(ya, yb, proj, h, wa, wb, wo, g1, b1, wr, br, alpha, tm, gacol, gbcol):
    R, D = h.shape
    C = ya.shape[1]
    const2 = lambda i: (0, 0)
    single = pl.Buffered(1)
    wr_hi = wr.astype(BF16)
    wr_split = jnp.concatenate([wr_hi, (wr - wr_hi.astype(F32)).astype(BF16)], axis=1)
    return pl.pallas_call(
        functools.partial(_merge_kernel, alpha=alpha),
        out_shape=(
            jax.ShapeDtypeStruct((R, D), F32),
            jax.ShapeDtypeStruct((R, D // 2), U32),
            jax.ShapeDtypeStruct((R, LANES), I32),
            jax.ShapeDtypeStruct((R, LANES), F32),
            jax.ShapeDtypeStruct((SUBLANES, LANES), F32),
        ),
        grid=(R // tm,),
        in_specs=[
            pl.BlockSpec((tm, C), lambda i: (i, 0)),
            pl.BlockSpec((tm, C), lambda i: (i, 0)),
            pl.BlockSpec((tm, D), lambda i: (i, gacol)),
            pl.BlockSpec((tm, D), lambda i: (i, gbcol)),
            pl.BlockSpec((tm, D), lambda i: (i, 0)),
            pl.BlockSpec((C, D), const2, pipeline_mode=single),
            pl.BlockSpec((C, D), const2, pipeline_mode=single),
            pl.BlockSpec((D, D), const2, pipeline_mode=single),
            pl.BlockSpec((1, D), const2),
            pl.BlockSpec((1, D), const2),
            pl.BlockSpec((D, 2 * LANES), const2, pipeline_mode=single),
            pl.BlockSpec((1, LANES), const2),
        ],
        out_specs=(
            pl.BlockSpec((tm, D), lambda i: (i, 0)),
            pl.BlockSpec((tm, D // 2), lambda i: (i, 0)),
            pl.BlockSpec((tm, LANES), lambda i: (i, 0)),
            pl.BlockSpec((tm, LANES), lambda i: (i, 0)),
            pl.BlockSpec((SUBLANES, LANES), const2),
        ),
        compiler_params=_params(("arbitrary",), 56),
        name="merge",
    )(ya, yb, proj, proj, h, wa, wb, wo, g1.reshape(1, D), b1.reshape(1, D), wr_split, br)


def _dest_kernel(idx_ref, start_ref, dest_ref, carry):
    i = pl.program_id(0)
    tr = idx_ref.shape[0]

    @pl.when(i == 0)
    def _():
        carry[...] = jnp.zeros(carry.shape, F32)

    idx = idx_ref[...]
    lane = lax.broadcasted_iota(I32, (tr, LANES), 1)
    ohs = [(lane == idx[:, j:j + 1]).astype(F32) for j in range(TOP_K)]
    oh = ohs[0] + ohs[1] + ohs[2] + ohs[3]
    ri = lax.broadcasted_iota(I32, (tr, tr), 0)
    ci = lax.broadcasted_iota(I32, (tr, tr), 1)
    stril = (ri > ci).astype(BF16)
    before = jnp.dot(stril, oh.astype(BF16), preferred_element_type=F32) + carry[0:1, :] + start_ref[...]
    dest = jnp.zeros((tr, LANES), I32)
    for j in range(TOP_K):
        pos = jnp.sum(ohs[j] * before, axis=-1, keepdims=True)
        dest = jnp.where(lane == j, pos.astype(I32), dest)
    dest_ref[...] = dest
    carry[0:1, :] = carry[0:1, :] + jnp.sum(oh, axis=0, keepdims=True)


def _dest(idx, starts, tr):
    R = idx.shape[0]
    return pl.pallas_call(
        _dest_kernel,
        out_shape=jax.ShapeDtypeStruct((R, LANES), I32),
        grid=(R // tr,),
        in_specs=[pl.BlockSpec((tr, LANES), lambda i: (i, 0)), pl.BlockSpec((1, LANES), lambda i: (0, 0))],
        out_specs=pl.BlockSpec((tr, LANES), lambda i: (i, 0)),
        scratch_shapes=[pltpu.VMEM((SUBLANES, LANES), F32)],
        compiler_params=_params(("arbitrary",), 32),
        name="moe_dest",
    )(idx, starts)


ROW_UNROLL = 4


def _dispatch_kernel(nv_ref, dest_ref, x_ref, xb_ref, zbuf, sem, zsem):
    tm = x_ref.shape[0]
    bm = zbuf.shape[0]
    n_blocks = xb_ref.shape[0] // bm

    @pl.when(pl.program_id(0) == 0)
    def _():
        zbuf[...] = jnp.zeros(zbuf.shape, U32)

        def zero_copy(b):
            return pltpu.make_async_copy(zbuf, xb_ref.at[pl.ds(pl.multiple_of(b * bm, bm), bm), :], zsem)

        def start(b, c):
            @pl.when(nv_ref[b] < bm)
            def _():
                zero_copy(b).start()
            return c

        def wait(b, c):
            @pl.when(nv_ref[b] < bm)
            def _():
                zero_copy(b).wait()
            return c

        lax.fori_loop(0, n_blocks, start, 0)
        lax.fori_loop(0, n_blocks, wait, 0)

    def row(r, c):
        for j in range(TOP_K):
            d = dest_ref[r * TOP_K + j]
            pltpu.make_async_copy(x_ref.at[pl.ds(r, 1), :], xb_ref.at[pl.ds(d, 1), :], sem).start(priority=j % 2)
        return c

    lax.fori_loop(0, tm, row, 0, unroll=ROW_UNROLL)
    for j in range(TOP_K):
        pltpu.make_async_copy(x_ref, xb_ref.at[pl.ds(0, tm), :], sem).wait()


def _dispatch(n_valid, dest_flat, hp, m_pad, tm, bm):
    R, W = hp.shape
    grid_spec = pltpu.PrefetchScalarGridSpec(
        num_scalar_prefetch=1,
        grid=(R // tm,),
        in_specs=[
            pl.BlockSpec((tm * TOP_K,), lambda i, nv: (i,), memory_space=pltpu.SMEM),
            pl.BlockSpec((tm, W), lambda i, nv: (i, 0)),
        ],
        out_specs=pl.BlockSpec(memory_space=pl.ANY),
        scratch_shapes=[pltpu.VMEM((bm, W), U32), pltpu.SemaphoreType.DMA, pltpu.SemaphoreType.DMA],
    )
    return pl.pallas_call(
        _dispatch_kernel,
        out_shape=jax.ShapeDtypeStruct((m_pad, W), U32),
        grid_spec=grid_spec,
        compiler_params=_params(("arbitrary",), 32),
        name="moe_dispatch",
    )(n_valid, dest_flat, hp)


def _deint_kernel(w_ref, p_ref, g_ref, l_ref):
    w = w_ref[0].astype(BF16)
    perm = p_ref[...]
    width = perm.shape[0]
    half = width // 2
    for g in range(w.shape[1] // width):
        t = jnp.dot(w[:, g * width:(g + 1) * width], perm, preferred_element_type=F32)
        g_ref[0, :, g * half:(g + 1) * half] = t[:, :half].astype(BF16)
        l_ref[0, :, g * half:(g + 1) * half] = t[:, half:].astype(BF16)


def _deinterleave(w_gu, tr):
    N, D, DE2 = w_gu.shape
    DE = DE2 // 2
    width = 2 * LANES
    src = lax.broadcasted_iota(I32, (width, width), 0)
    dst = lax.broadcasted_iota(I32, (width, width), 1)
    perm = (src == jnp.where(dst < LANES, 2 * dst, 2 * (dst - LANES) + 1)).astype(BF16)
    out = jax.ShapeDtypeStruct((N, D, DE), BF16)
    return pl.pallas_call(
        _deint_kernel,
        out_shape=(out, out),
        grid=(N, D // tr),
        in_specs=[
            pl.BlockSpec((1, tr, DE2), lambda n, i: (n, i, 0)),
            pl.BlockSpec((width, width), lambda n, i: (0, 0)),
        ],
        out_specs=(pl.BlockSpec((1, tr, DE), lambda n, i: (n, i, 0)),
                   pl.BlockSpec((1, tr, DE), lambda n, i: (n, i, 0))),
        compiler_params=_params(("arbitrary", "arbitrary"), 40),
        name="deinterleave",
    )(w_gu, perm)


FFN_GATES = 4


def _ffn_kernel(be_ref, nv_ref, nu_ref, x_ref, wg_ref, bg_ref, wl_ref, bl_ref, wd_ref, bd_ref, o_ref):
    del be_ref
    b = pl.program_id(0)
    tm = x_ref.shape[0]
    q = tm // FFN_GATES
    used = b < nu_ref[0]
    gates_needed = jnp.maximum((nv_ref[b] + q - 1) // q, 1)

    def ffn_rows(rows):
        x = _unpack_pairs(x_ref[rows, :])
        glu = jnp.dot(x, wg_ref[0], preferred_element_type=F32) + bg_ref[0]
        lin = jnp.dot(x, wl_ref[0], preferred_element_type=F32) + bl_ref[0]
        glu = jnp.minimum(glu, SWIGLU_LIMIT)
        lin = jnp.clip(lin, -SWIGLU_LIMIT, SWIGLU_LIMIT)
        act = glu * _sigmoid(SWIGLU_ALPHA * glu) * (lin + 1.0)
        y = jnp.dot(act.astype(BF16), wd_ref[0], preferred_element_type=F32) + bd_ref[0]
        o_ref[rows, :] = _pack_pairs(y)

    @pl.when(jnp.logical_not(used))
    def _():
        o_ref[...] = jnp.zeros(o_ref.shape, U32)

    for k in range(1, FFN_GATES + 1):
        @pl.when(jnp.logical_and(used, gates_needed == k))
        def _(k=k):
            ffn_rows(slice(tm - k * q, tm))
            if k < FFN_GATES:
                o_ref[:tm - k * q, :] = jnp.zeros((tm - k * q, o_ref.shape[1]), U32)


def _ffn(block_exp, n_valid, n_used, xb, wg, bg, wl, bl, wd, bd, tm):
    m_pad, W = xb.shape
    _, D, DE = wg.shape
    n_blocks = m_pad // tm

    def rows(b, be, nv, nu):
        return (jnp.minimum(b, nu[0] - 1), 0)

    def expert(b, be, nv, nu):
        return (be[jnp.minimum(b, nu[0] - 1)], 0, 0)

    grid_spec = pltpu.PrefetchScalarGridSpec(
        num_scalar_prefetch=3,
        grid=(n_blocks,),
        in_specs=[
            pl.BlockSpec((tm, W), rows),
            pl.BlockSpec((1, D, DE), expert),
            pl.BlockSpec((1, 1, DE), expert),
            pl.BlockSpec((1, D, DE), expert),
            pl.BlockSpec((1, 1, DE), expert),
            pl.BlockSpec((1, DE, D), expert),
            pl.BlockSpec((1, 1, D), expert),
        ],
        out_specs=pl.BlockSpec((tm, W), lambda b, be, nv, nu: (b, 0)),
    )
    return pl.pallas_call(
        _ffn_kernel,
        out_shape=jax.ShapeDtypeStruct((m_pad, W), U32),
        grid_spec=grid_spec,
        compiler_params=_params(("arbitrary",), 56),
        name="moe_ffn",
    )(block_exp, n_valid, n_used, xb, wg, bg, wl, bl, wd, bd)


def _combine_kernel(dest_ref, dnext_ref, gate_ref, h_ref, g2_ref, b2_ref, yb_ref, of_ref, ob_ref, buf, sems,
                    *, alpha):
    i = pl.program_id(0)
    tm, D = h_ref.shape
    slot = i % 2

    def gather(d_ref, s):
        def row(r, c):
            for j in range(TOP_K):
                d = d_ref[r * TOP_K + j]
                pltpu.make_async_copy(
                    yb_ref.at[pl.ds(d, 1), :], buf.at[s, j, pl.ds(r, 1), :], sems.at[s]).start(priority=j % 2)
            return c

        lax.fori_loop(0, tm, row, 0, unroll=ROW_UNROLL)

    @pl.when(i == 0)
    def _():
        gather(dest_ref, slot)

    @pl.when(i + 1 < pl.num_programs(0))
    def _():
        gather(dnext_ref, 1 - slot)

    for j in range(TOP_K):
        pltpu.make_async_copy(yb_ref.at[pl.ds(0, tm), :], buf.at[slot, j], sems.at[slot]).wait()

    gate = gate_ref[...]
    moe = None
    for j in range(TOP_K):
        term = gate[:, j:j + 1] * _unpack_pairs_f32(buf[slot, j])
        moe = term if moe is None else moe + term
    y = _layer_norm(alpha * h_ref[...] + moe, g2_ref[...], b2_ref[...])
    of_ref[...] = y
    ob_ref[...] = y.astype(BF16)


def _combine(dest_flat, gate, h, g2, b2, yb, alpha, tm):
    R, D = h.shape
    W = yb.shape[1]
    n = R // tm
    row = pl.BlockSpec((tm, D), lambda i: (i, 0))
    vec = pl.BlockSpec((1, D), lambda i: (0, 0))
    return pl.pallas_call(
        functools.partial(_combine_kernel, alpha=alpha),
        out_shape=(jax.ShapeDtypeStruct((R, D), F32), jax.ShapeDtypeStruct((R, D), BF16)),
        grid=(n,),
        in_specs=[
            pl.BlockSpec((tm * TOP_K,), lambda i: (i,), memory_space=pltpu.SMEM),
            pl.BlockSpec((tm * TOP_K,), lambda i: (jnp.minimum(i + 1, n - 1),), memory_space=pltpu.SMEM),
            pl.BlockSpec((tm, LANES), lambda i: (i, 0)),
            row, vec, vec,
            pl.BlockSpec(memory_space=pl.ANY),
        ],
        out_specs=(row, row),
        scratch_shapes=[pltpu.VMEM((2, TOP_K, tm, W), U32), pltpu.SemaphoreType.DMA((2,))],
        compiler_params=_params(("arbitrary",), 48),
        name="moe_combine",
    )(dest_flat, dest_flat, gate, h, g2.reshape(1, D), b2.reshape(1, D), yb)


def _lane_pad(v, offset):
    return jnp.zeros((1, LANES), F32).at[0, offset:offset + v.shape[0]].set(v.astype(F32))


def kernel(x, meta_tokens, ln_in_g, ln_in_b, w_in, rg_conv_w, rg_conv_b, rg_wx, rg_bx, rg_wa, rg_ba, rg_lambda, gdn_conv_w, gdn_a_log, gdn_dt_bias, gdn_norm_w, w_branch_a, w_branch_b, w_out, ln1_g, ln1_b, w_router, b_router, w_gate_up, b_gate_up, w_down, b_down, ln2_g, ln2_b):
    B, seq, D = x.shape
    depth = w_in.shape[0]
    E = w_router.shape[2]
    DE = w_down.shape[2]
    d_rnn = D // 2
    assert D == 2 * GDN_V_DIM and d_rnn == GDN_V_DIM and E <= LANES
    alpha = float((2 * depth) ** 0.25)

    T = N_META + seq
    pad = (-T) % CHUNK
    Tp = T + pad
    R = B * Tp

    tt_ln = _pick_tile(Tp, 832, CHUNK)
    tm_in = _pick_tile(R, 1040)
    tn_in = 1024
    tt_rg = _pick_tile(Tp, 416, SUBLANES)
    tt_gdn = _pick_tile(Tp, 320, CHUNK)
    tm_mg = _pick_tile(R, 256)
    tm_moe = _pick_tile(R, 256)
    bm_ffn = 2 * tm_moe
    n_blocks = -(-(R * TOP_K) // bm_ffn) + E
    m_pad = n_blocks * bm_ffn

    h, hb = _ln_in(x, meta_tokens, ln_in_g, ln_in_b, pad, tt_ln)

    wg_all, wl_all = _deinterleave(w_gate_up.reshape(depth * E, D, 2 * DE), _pick_tile(D, 512))
    wd_all = w_down.reshape(depth * E, DE, D).astype(BF16)
    bg_all = b_gate_up.reshape(depth * E, 1, 2 * DE)[:, :, 0::2]
    bl_all = b_gate_up.reshape(depth * E, 1, 2 * DE)[:, :, 1::2]
    bd_all = b_down.reshape(depth * E, 1, D)

    o_rx = 0
    o_q = 2 * d_rnn
    o_z = o_q + 2 * GDN_QK_DIM + GDN_V_DIM
    o_beta = o_z + GDN_V_DIM
    o_ga = o_beta + 2 * GDN_H_V
    o_gb = o_ga + D
    block_start = jnp.arange(n_blocks, dtype=I32) * bm_ffn

    for l in range(depth):
        wl = w_in[l]
        w_main = jnp.concatenate(
            [wl[:, o_ga:o_ga + D], wl[:, o_gb:o_gb + D], wl[:, o_q:o_z], wl[:, o_rx:o_q], wl[:, o_z:o_beta]],
            axis=1).astype(BF16)
        w_small = jnp.zeros((D, LANES), F32).at[:, :2 * GDN_H_V].set(wl[:, o_beta:o_ga]).astype(BF16)
        proj, small = _inproj(hb, w_main, w_small, tm_in, tn_in)

        ya = _rglru(proj, rg_conv_w[l], rg_conv_b[l], rg_wx[l].astype(BF16), rg_bx[l], rg_wa[l].astype(BF16),
                    rg_ba[l], rg_lambda[l], B, Tp, pad, tt_rg, xcol=(2 * D + 2048) // d_rnn,
                    ycol=(2 * D + 2048) // d_rnn + 1)
        yb = _gdn(proj, small, gdn_conv_w[l], _lane_pad(gdn_a_log[l], GDN_H_V), _lane_pad(gdn_dt_bias[l], GDN_H_V),
                  gdn_norm_w[l].reshape(1, GDN_DV).astype(F32), B, Tp, pad, tt_gdn, qkvcol=2,
                  zcol=(2 * D + 2048 + 2 * d_rnn) // GDN_V_DIM)

        wr = jnp.zeros((D, LANES), F32).at[:, :E].set(w_router[l])
        br = jnp.full((1, LANES), -jnp.inf, F32).at[0, :E].set(b_router[l])
        hn, hp, idx, gate, counts = _merge(
            ya, yb, proj, h, w_branch_a[l].astype(BF16), w_branch_b[l].astype(BF16), w_out[l].astype(BF16),
            ln1_g[l], ln1_b[l], wr, br, alpha, tm_mg, gacol=0, gbcol=1)

        cnt = counts[0, :E].astype(I32)
        padded = (cnt + bm_ffn - 1) // bm_ffn * bm_ffn
        pad_ends = jnp.cumsum(padded)
        first_row = pad_ends - cnt
        block_exp = jnp.minimum(jnp.sum((pad_ends[None, :] <= block_start[:, None]).astype(I32), axis=1), E - 1)
        onehot = (block_exp[:, None] == jnp.arange(E, dtype=I32)[None, :]).astype(I32)
        n_valid = jnp.clip(block_start + bm_ffn - jnp.sum(onehot * first_row[None, :], axis=1), 0, bm_ffn)
        n_valid = jnp.where(block_start < pad_ends[E - 1], n_valid, 0).astype(I32)
        n_used = (pad_ends[E - 1:E] // bm_ffn).astype(I32)

        dest = _dest(idx, _lane_pad(first_row, 0), tm_moe)
        dest_flat = dest[:, :TOP_K].reshape(R * TOP_K)

        xb = _dispatch(n_valid, dest_flat, hp, m_pad, tm_moe, bm_ffn)
        ybk = _ffn(block_exp + l * E, n_valid, n_used, xb, wg_all, bg_all, wl_all, bl_all, wd_all, bd_all, bm_ffn)
        h, hb = _combine(dest_flat, gate, hn, ln2_g[l], ln2_b[l], ybk, alpha, tm_moe)

    return h.reshape(B, Tp, D)[:, pad + N_META:]
```

```python
import functools

import jax
import jax.numpy as jnp
from jax import lax
from jax.experimental import pallas as pl
from jax.experimental.pallas import tpu as pltpu

F32 = jnp.float32
BF16 = jnp.bfloat16
I32 = jnp.int32
U32 = jnp.uint32

N_META = 16
CONV_W = 4
CHUNK = 64
TOP_K = 4
RG_BLOCKS = 8
RG_C = 8.0
GDN_DK = 128
GDN_DV = 128
GDN_H_QK = 4
GDN_H_V = 8
GDN_QK_DIM = GDN_H_QK * GDN_DK
GDN_V_DIM = GDN_H_V * GDN_DV
SWIGLU_LIMIT = 7.0
SWIGLU_ALPHA = 1.702
LN_EPS = 1e-5
RMS_EPS = 1e-6

LANES = 128
SUBLANES = 8
MIB = 1024 * 1024
HIGHEST = lax.Precision.HIGHEST


def _pick_tile(n, target, mult=16):
    best = None
    for t in range(mult, min(n, target) + 1, mult):
        if n % t == 0:
            best = t
    assert best is not None, (n, target, mult)
    return best


def _params(sem, vmem_mib):
    return pltpu.CompilerParams(dimension_semantics=sem, vmem_limit_bytes=vmem_mib * MIB)


def _layer_norm(x, g, b):
    mu = jnp.mean(x, axis=-1, keepdims=True)
    xc = x - mu
    var = jnp.mean(xc * xc, axis=-1, keepdims=True)
    return xc * lax.rsqrt(var + LN_EPS) * g + b


def _sigmoid(x):
    return 1.0 / (1.0 + jnp.exp(-x))


def _softplus(x):
    return jnp.maximum(x, 0.0) + jnp.log1p(jnp.exp(-jnp.abs(x)))


def _silu(x):
    return x * _sigmoid(x)


def _gelu_tanh(x):
    return 0.5 * x * (1.0 + jnp.tanh(0.7978845608028654 * (x + 0.044715 * x * x * x)))


def _pack_pairs(x):
    n = x.shape[1] // 2
    xb = x.astype(BF16).astype(F32)
    lo = lax.bitcast_convert_type(xb[:, :n], U32) >> 16
    hi = lax.bitcast_convert_type(xb[:, n:], U32) & jnp.uint32(0xFFFF0000)
    return lo | hi


def _unpack_pairs_f32(w):
    lo = lax.bitcast_convert_type(w << 16, F32)
    hi = lax.bitcast_convert_type(w & jnp.uint32(0xFFFF0000), F32)
    return jnp.concatenate([lo, hi], axis=1)


def _unpack_pairs(w):
    return _unpack_pairs_f32(w).astype(BF16)


def _ln_in_kernel(x_ref, meta_ref, g_ref, b_ref, of_ref, ob_ref, xin, sem, *, pad):
    bi = pl.program_id(0)
    t = pl.program_id(1)
    tt = xin.shape[0]
    lead = pad + meta_ref.shape[0]

    @pl.when(t == 0)
    def _():
        if pad:
            xin[0:pad, :] = jnp.zeros((pad, xin.shape[1]), F32)
        xin[pad:lead, :] = meta_ref[...]
        if tt > lead:
            cp = pltpu.make_async_copy(
                x_ref.at[bi, pl.ds(0, tt - lead), :], xin.at[pl.ds(lead, tt - lead), :], sem)
            cp.start()
            cp.wait()

    @pl.when(t > 0)
    def _():
        start = pl.multiple_of(t * tt - lead, SUBLANES)
        cp = pltpu.make_async_copy(x_ref.at[bi, pl.ds(start, tt), :], xin, sem)
        cp.start()
        cp.wait()

    y = _layer_norm(xin[...], g_ref[...], b_ref[...])
    of_ref[...] = y
    ob_ref[...] = y.astype(BF16)


def _ln_in(x, meta, g, b, pad, tt):
    B, seq, D = x.shape
    lead = pad + meta.shape[0]
    Tp = seq + lead
    nT = Tp // tt
    assert lead % SUBLANES == 0 and lead <= tt and Tp % tt == 0
    row = pl.BlockSpec((tt, D), lambda bi, t: (bi * nT + t, 0))
    vec = pl.BlockSpec((1, D), lambda bi, t: (0, 0))
    return pl.pallas_call(
        functools.partial(_ln_in_kernel, pad=pad),
        out_shape=(jax.ShapeDtypeStruct((B * Tp, D), F32), jax.ShapeDtypeStruct((B * Tp, D), BF16)),
        grid=(B, nT),
        in_specs=[pl.BlockSpec(memory_space=pl.ANY), pl.BlockSpec(meta.shape, lambda bi, t: (0, 0)), vec, vec],
        out_specs=(row, row),
        scratch_shapes=[pltpu.VMEM((tt, D), F32), pltpu.SemaphoreType.DMA],
        compiler_params=_params(("arbitrary", "arbitrary"), 48),
        name="ln_in",
    )(x, meta.astype(F32), g.reshape(1, D), b.reshape(1, D))


def _inproj_kernel(x_ref, w_ref, ws_ref, o_ref, os_ref):
    x = x_ref[...]
    o_ref[...] = jnp.dot(x, w_ref[...], preferred_element_type=F32)

    @pl.when(pl.program_id(1) == 0)
    def _():
        os_ref[...] = jnp.dot(x, ws_ref[...], preferred_element_type=F32)


def _inproj(hb, w_main, w_small, tm, tn):
    R, D = hb.shape
    N = w_main.shape[1]
    return pl.pallas_call(
        _inproj_kernel,
        out_shape=(jax.ShapeDtypeStruct((R, N), F32), jax.ShapeDtypeStruct((R, LANES), F32)),
        grid=(R // tm, N // tn),
        in_specs=[
            pl.BlockSpec((tm, D), lambda i, j: (i, 0)),
            pl.BlockSpec((D, tn), lambda i, j: (0, j)),
            pl.BlockSpec((D, LANES), lambda i, j: (0, 0)),
        ],
        out_specs=(
            pl.BlockSpec((tm, tn), lambda i, j: (i, j)),
            pl.BlockSpec((tm, LANES), lambda i, j: (i, 0)),
        ),
        compiler_params=_params(("arbitrary", "arbitrary"), 48),
        name="inproj",
    )(hb, w_main, w_small)


def _causal_conv(x, xbuf, cw_ref, tt):
    xbuf[SUBLANES:SUBLANES + tt, :] = x
    y = cw_ref[CONV_W - 1:CONV_W, :] * x
    for j in range(CONV_W - 1):
        off = SUBLANES - (CONV_W - 1) + j
        y = y + cw_ref[j:j + 1, :] * xbuf[off:off + tt, :]
    xbuf[0:SUBLANES, :] = xbuf[tt:tt + SUBLANES, :]
    return y


def _rglru_kernel(x_ref, y_ref, cw_ref, cb_ref, wx_ref, bx_ref, wa_ref, ba_ref, lam_ref,
                  o_ref, xbuf, a_s, u_s, hc, *, pad):
    t = pl.program_id(1)
    tt, C = x_ref.shape
    bs = C // RG_BLOCKS

    @pl.when(t == 0)
    def _():
        xbuf[0:SUBLANES, :] = jnp.zeros((SUBLANES, C), F32)
        hc[...] = jnp.zeros((SUBLANES, C), F32)

    row = lax.broadcasted_iota(I32, (tt, 1), 0)
    valid = (row + t * tt) >= pad
    x = jnp.where(valid, x_ref[...], 0.0)
    xc = _causal_conv(x, xbuf, cw_ref, tt) + cb_ref[...]

    xcb = xc.astype(BF16)
    gi = jnp.concatenate(
        [jnp.dot(xcb[:, h * bs:(h + 1) * bs], wx_ref[h], preferred_element_type=F32) for h in range(RG_BLOCKS)],
        axis=1)
    gr = jnp.concatenate(
        [jnp.dot(xcb[:, h * bs:(h + 1) * bs], wa_ref[h], preferred_element_type=F32) for h in range(RG_BLOCKS)],
        axis=1)
    gi = _sigmoid(gi + bx_ref[...])
    gr = _sigmoid(gr + ba_ref[...])
    log_a = (-RG_C) * gr * _softplus(-lam_ref[...])
    a = jnp.exp(log_a)
    u = jnp.sqrt(-jnp.tanh(log_a) * (a * a + 1.0)) * gi * xc
    u = jnp.where(valid, u, 0.0)

    groups = tt // SUBLANES
    a = a.reshape(groups, SUBLANES, C)
    u = u.reshape(groups, SUBLANES, C)
    rowmod = lax.broadcasted_iota(I32, (groups, SUBLANES, 1), 1)
    for d in (1, 2, 4):
        m = rowmod >= d
        a_sh = pltpu.roll(a, d, 1)
        u_sh = pltpu.roll(u, d, 1)
        u = jnp.where(m, a * u_sh + u, u)
        a = jnp.where(m, a * a_sh, a)
    a_s[...] = a.reshape(tt, C)
    u_s[...] = u.reshape(tt, C)

    def group(g, hprev):
        r0 = pl.multiple_of(g * SUBLANES, SUBLANES)
        h = a_s[pl.ds(r0, SUBLANES), :] * hprev + u_s[pl.ds(r0, SUBLANES), :]
        u_s[pl.ds(r0, SUBLANES), :] = h
        return jnp.broadcast_to(h[SUBLANES - 1:SUBLANES, :], (SUBLANES, C))

    hc[...] = lax.fori_loop(0, tt // SUBLANES, group, hc[...])
    o_ref[...] = (u_s[...] * _gelu_tanh(y_ref[...])).astype(BF16)


def _rglru(proj, cw, cb, wx, bx, wa, ba, lam, B, Tp, pad, tt, xcol, ycol):
    R = proj.shape[0]
    C = cw.shape[1]
    nT = Tp // tt
    vec = pl.BlockSpec((1, C), lambda b, t: (0, 0))
    wspec = pl.BlockSpec(wx.shape, lambda b, t: (0, 0, 0))
    return pl.pallas_call(
        functools.partial(_rglru_kernel, pad=pad),
        out_shape=jax.ShapeDtypeStruct((R, C), BF16),
        grid=(B, nT),
        in_specs=[
            pl.BlockSpec((tt, C), lambda b, t: (b * nT + t, xcol)),
            pl.BlockSpec((tt, C), lambda b, t: (b * nT + t, ycol)),
            pl.BlockSpec((CONV_W, C), lambda b, t: (0, 0)),
            vec, wspec, vec, wspec, vec, vec,
        ],
        out_specs=pl.BlockSpec((tt, C), lambda b, t: (b * nT + t, 0)),
        scratch_shapes=[
            pltpu.VMEM((tt + SUBLANES, C), F32),
            pltpu.VMEM((tt, C), F32),
            pltpu.VMEM((tt, C), F32),
            pltpu.VMEM((SUBLANES, C), F32),
        ],
        compiler_params=_params(("arbitrary", "arbitrary"), 48),
        name="rglru",
    )(proj, proj, cw, cb.reshape(1, C), wx, bx.reshape(1, C), wa, ba.reshape(1, C), lam.reshape(1, C))


def _bdot(a, b):
    return jnp.dot(a.astype(BF16), b.astype(BF16), preferred_element_type=F32)


def _bdot_nt(a, b):
    return lax.dot_general(a.astype(BF16), b.astype(BF16), (((1,), (1,)), ((), ())), preferred_element_type=F32)


def _bdot_tn(a, b):
    return lax.dot_general(a.astype(BF16), b.astype(BF16), (((0,), (0,)), ((), ())), preferred_element_type=F32)


def _unit_lower_inverses(Ls, masks):
    eye = masks[0]
    dinvs = [eye - L * masks[1] for L in Ls]
    for m in masks[2:]:
        ts = [_bdot(d, L * m) for d, L in zip(dinvs, Ls)]
        dinvs = [d - _bdot(t, d) for d, t in zip(dinvs, ts)]
    return dinvs


def _gdn_kernel(qkv_ref, z_ref, sm_ref, cw_ref, alog_ref, dtb_ref, nw_ref,
                o_ref, xbuf, q_s, k_s, v_s, b_s, g_s, S_ref, val_s, kq_s, at_s, kd_s, eg_s, *, pad):
    t = pl.program_id(1)
    tt, CQ = qkv_ref.shape
    ncht = tt // CHUNK
    rep = GDN_H_V // GDN_H_QK

    @pl.when(t == 0)
    def _():
        xbuf[0:SUBLANES, :] = jnp.zeros((SUBLANES, CQ), F32)
        S_ref[...] = jnp.zeros(S_ref.shape, F32)

    row = lax.broadcasted_iota(I32, (tt, 1), 0)
    valid = (row + t * tt) >= pad
    x = jnp.where(valid, qkv_ref[...], 0.0)
    xc = _silu(_causal_conv(x, xbuf, cw_ref, tt))

    for hq in range(GDN_H_QK):
        qh = xc[:, hq * GDN_DK:(hq + 1) * GDN_DK]
        kh = xc[:, GDN_QK_DIM + hq * GDN_DK:GDN_QK_DIM + (hq + 1) * GDN_DK]
        qn = qh * lax.rsqrt(jnp.sum(qh * qh, axis=-1, keepdims=True) + RMS_EPS) * (GDN_DK ** -0.5)
        kn = kh * lax.rsqrt(jnp.sum(kh * kh, axis=-1, keepdims=True) + RMS_EPS)
        q_s[:, hq * GDN_DK:(hq + 1) * GDN_DK] = qn
        k_s[:, hq * GDN_DK:(hq + 1) * GDN_DK] = kn
    v_s[...] = xc[:, 2 * GDN_QK_DIM:]

    sm = sm_ref[...]
    b_s[...] = _sigmoid(sm)
    g = -jnp.exp(alog_ref[...]) * _softplus(sm + dtb_ref[...])
    g_s[...] = jnp.where(valid, g, 0.0)

    ri = lax.broadcasted_iota(I32, (CHUNK, CHUNK), 0)
    ci = lax.broadcasted_iota(I32, (CHUNK, CHUNK), 1)
    causal = ri >= ci
    strict = ri > ci
    tril = causal.astype(F32)
    masks = [(ri == ci).astype(F32)]
    s = 1
    while s < CHUNK:
        masks.append(((ri // (2 * s) == ci // (2 * s)) & (ri % (2 * s) >= s) & (ci % (2 * s) < s)).astype(F32))
        s *= 2
    nw = nw_ref[...]

    heads = range(GDN_H_V)
    chunks = range(ncht)
    inst = [(c, h) for c in chunks for h in heads]

    rows = [slice(c * CHUNK, (c + 1) * CHUNK) for c in chunks]
    G = [jnp.dot(tril, g_s[rows[c], :], precision=HIGHEST, preferred_element_type=F32) for c in chunks]
    GT = [G[c].T for c in chunks]
    bet = [b_s[rows[c], :] for c in chunks]
    qs = [[q_s[rows[c], hq * GDN_DK:(hq + 1) * GDN_DK] for hq in range(GDN_H_QK)] for c in chunks]
    ks_ = [[k_s[rows[c], hq * GDN_DK:(hq + 1) * GDN_DK] for hq in range(GDN_H_QK)] for c in chunks]
    kk = [[_bdot_nt(k, k) for k in ks_[c]] for c in chunks]
    qk = [[_bdot_nt(q, k) for q, k in zip(qs[c], ks_[c])] for c in chunks]
    bcol = {(c, h): bet[c][:, h:h + 1] for c, h in inst}
    Gc = {(c, h): G[c][:, GDN_H_V + h:GDN_H_V + h + 1] for c, h in inst}
    decay = {(c, h): jnp.where(causal, jnp.exp(jnp.minimum(
        Gc[c, h] - GT[c][GDN_H_V + h:GDN_H_V + h + 1, :], 0.0)), 0.0) for c, h in inst}
    eG = {i: jnp.exp(Gc[i]) for i in inst}
    g_last = {i: Gc[i][CHUNK - 1:CHUNK, :] for i in inst}
    Ls = [jnp.where(strict, kk[c][h // rep] * bcol[c, h] * decay[c, h], 0.0) for c, h in inst]
    Ts = dict(zip(inst, _unit_lower_inverses(Ls, masks)))
    for c, h in inst:
        k = ks_[c][h // rep]
        sol = _bdot(Ts[c, h], jnp.concatenate(
            [v_s[rows[c], h * GDN_DV:(h + 1) * GDN_DV] * bcol[c, h], k * (bcol[c, h] * eG[c, h])], axis=1))
        val_s[c, h] = sol[:, :GDN_DV]
        kq_s[c, h] = jnp.concatenate([sol[:, GDN_DV:], qs[c][h // rep] * eG[c, h]], axis=0).astype(BF16)
        at_s[c, h] = (qk[c][h // rep] * decay[c, h]).astype(BF16)
        kd_s[c, h] = (k * jnp.exp(g_last[c, h] - Gc[c, h])).astype(BF16)
        eg_s[c, h] = jnp.broadcast_to(jnp.exp(g_last[c, h]), (SUBLANES, GDN_DV))

    for c in chunks:
        zc = z_ref[rows[c], :]
        Sold = [S_ref[h] for h in heads]
        PS = [_bdot(kq_s[c, h], Sold[h]) for h in heads]
        v_new = [val_s[c, h] - PS[h][:CHUNK] for h in heads]
        o = [PS[h][CHUNK:] + _bdot(at_s[c, h], v_new[h]) for h in heads]
        for h in heads:
            S_ref[h] = Sold[h] * eg_s[c, h][0:1, :] + _bdot_tn(kd_s[c, h], v_new[h])
        for h in heads:
            oh = o[h] * lax.rsqrt(jnp.mean(o[h] * o[h], axis=-1, keepdims=True) + RMS_EPS) * nw
            oh = oh * _silu(zc[:, h * GDN_DV:(h + 1) * GDN_DV])
            o_ref[rows[c], h * GDN_DV:(h + 1) * GDN_DV] = oh.astype(BF16)


def _gdn(proj, small, cw, alog_pad, dtb_pad, nw, B, Tp, pad, tt, qkvcol, zcol):
    R = proj.shape[0]
    CQ = cw.shape[1]
    nT = Tp // tt
    lane_vec = pl.BlockSpec((1, LANES), lambda b, t: (0, 0))
    return pl.pallas_call(
        functools.partial(_gdn_kernel, pad=pad),
        out_shape=jax.ShapeDtypeStruct((R, GDN_V_DIM), BF16),
        grid=(B, nT),
        in_specs=[
            pl.BlockSpec((tt, CQ), lambda b, t: (b * nT + t, qkvcol)),
            pl.BlockSpec((tt, GDN_V_DIM), lambda b, t: (b * nT + t, zcol)),
            pl.BlockSpec((tt, LANES), lambda b, t: (b * nT + t, 0)),
            pl.BlockSpec((CONV_W, CQ), lambda b, t: (0, 0)),
            lane_vec, lane_vec, lane_vec,
        ],
        out_specs=pl.BlockSpec((tt, GDN_V_DIM), lambda b, t: (b * nT + t, 0)),
        scratch_shapes=[
            pltpu.VMEM((tt + SUBLANES, CQ), F32),
            pltpu.VMEM((tt, GDN_QK_DIM), F32),
            pltpu.VMEM((tt, GDN_QK_DIM), F32),
            pltpu.VMEM((tt, GDN_V_DIM), F32),
            pltpu.VMEM((tt, LANES), F32),
            pltpu.VMEM((tt, LANES), F32),
            pltpu.VMEM((GDN_H_V, GDN_DK, GDN_DV), F32),
            pltpu.VMEM((tt // CHUNK, GDN_H_V, CHUNK, GDN_DV), F32),
            pltpu.VMEM((tt // CHUNK, GDN_H_V, 2 * CHUNK, GDN_DK), BF16),
            pltpu.VMEM((tt // CHUNK, GDN_H_V, CHUNK, CHUNK), BF16),
            pltpu.VMEM((tt // CHUNK, GDN_H_V, CHUNK, GDN_DK), BF16),
            pltpu.VMEM((tt // CHUNK, GDN_H_V, SUBLANES, GDN_DV), F32),
        ],
        compiler_params=_params(("arbitrary", "arbitrary"), 48),
        name="gdn",
    )(proj, proj, small, cw, alog_pad, dtb_pad, nw)


MERGE_SUBTILES = 2


def _merge_kernel(ya_ref, yb_ref, ga_ref, gb_ref, h_ref, wa_ref, wb_ref, wo_ref, g1_ref, b1_ref,
                  wr_ref, br_ref, hf_ref, hp_ref, idx_ref, gate_ref, cnt_ref, *, alpha):
    i = pl.program_id(0)
    tm, D = h_ref.shape
    sub = tm // MERGE_SUBTILES
    lane = lax.broadcasted_iota(I32, (sub, LANES), 1)
    counts = jnp.zeros((1, LANES), F32)
    subs = range(MERGE_SUBTILES)
    rows_of = [slice(s * sub, (s + 1) * sub) for s in subs]
    branch = [(jnp.dot(ya_ref[r, :], wa_ref[...], preferred_element_type=F32),
               jnp.dot(yb_ref[r, :], wb_ref[...], preferred_element_type=F32)) for r in rows_of]
    outs = []
    for s in subs:
        r = rows_of[s]
        mixed = _sigmoid(ga_ref[r, :]) * branch[s][0] + _sigmoid(gb_ref[r, :]) * branch[s][1]
        outs.append(jnp.dot(mixed.astype(BF16), wo_ref[...], preferred_element_type=F32))
    logits = []
    for s in subs:
        r = rows_of[s]
        hn = _layer_norm(alpha * h_ref[r, :] + outs[s], g1_ref[...], b1_ref[...])
        hf_ref[r, :] = hn
        hp_ref[r, :] = _pack_pairs(hn)
        hn_hi = hn.astype(BF16)
        hn_lo = (hn - hn_hi.astype(F32)).astype(BF16)
        p_hi = jnp.dot(hn_hi, wr_ref[...], preferred_element_type=F32)
        p_lo = jnp.dot(hn_lo, wr_ref[:, :LANES], preferred_element_type=F32)
        logits.append(p_hi[:, :LANES] + p_hi[:, LANES:] + p_lo + br_ref[...])
    vals = list(logits)
    tops = [[] for _ in subs]
    sels = [[] for _ in subs]
    idx_outs = [jnp.zeros((sub, LANES), I32) for _ in subs]
    for j in range(TOP_K):
        ms = [jnp.max(vals[s], axis=-1, keepdims=True) for s in subs]
        idxs = [jnp.min(jnp.where(vals[s] == ms[s], lane, LANES), axis=-1, keepdims=True) for s in subs]
        for s in subs:
            sel = lane == idxs[s]
            tops[s].append(ms[s])
            sels[s].append(sel)
            idx_outs[s] = jnp.where(lane == j, idxs[s], idx_outs[s])
            vals[s] = jnp.where(sel, -jnp.inf, vals[s])
    for s in subs:
        rows = rows_of[s]
        idx_out = idx_outs[s]
        es = [jnp.exp(m - tops[s][0]) for m in tops[s]]
        inv = 1.0 / (es[0] + es[1] + es[2] + es[3])
        gate_out = jnp.zeros((sub, LANES), F32)
        onehot = jnp.zeros((sub, LANES), F32)
        for j in range(TOP_K):
            gate_out = jnp.where(lane == j, es[j] * inv, gate_out)
            onehot = onehot + sels[s][j].astype(F32)
        idx_ref[rows, :] = idx_out
        gate_ref[rows, :] = gate_out
        counts = counts + jnp.sum(onehot, axis=0, keepdims=True)

    @pl.when(i == 0)
    def _():
        cnt_ref[...] = jnp.zeros(cnt_ref.shape, F32)

    cnt_ref[0:1, :] = cnt_ref[0:1, :] + counts


def _merge(ya, yb, proj, h, wa, wb, wo, g1, b1, wr, br, alpha, tm, gacol, gbcol):
    R, D = h.shape
    C = ya.shape[1]
    const2 = lambda i: (0, 0)
    single = pl.Buffered(1)
    wr_hi = wr.astype(BF16)
    wr_split = jnp.concatenate([wr_hi, (wr - wr_hi.astype(F32)).astype(BF16)], axis=1)
    return pl.pallas_call(
        functools.partial(_merge_kernel, alpha=alpha),
        out_shape=(
            jax.ShapeDtypeStruct((R, D), F32),
            jax.ShapeDtypeStruct((R, D // 2), U32),
            jax.ShapeDtypeStruct((R, LANES), I32),
            jax.ShapeDtypeStruct((R, LANES), F32),
            jax.ShapeDtypeStruct((SUBLANES, LANES), F32),
        ),
        grid=(R // tm,),
        in_specs=[
            pl.BlockSpec((tm, C), lambda i: (i, 0)),
            pl.BlockSpec((tm, C), lambda i: (i, 0)),
            pl.BlockSpec((tm, D), lambda i: (i, gacol)),
            pl.BlockSpec((tm, D), lambda i: (i, gbcol)),
            pl.BlockSpec((tm, D), lambda i: (i, 0)),
            pl.BlockSpec((C, D), const2, pipeline_mode=single),
            pl.BlockSpec((C, D), const2, pipeline_mode=single),
            pl.BlockSpec((D, D), const2, pipeline_mode=single),
            pl.BlockSpec((1, D), const2),
            pl.BlockSpec((1, D), const2),
            pl.BlockSpec((D, 2 * LANES), const2, pipeline_mode=single),
            pl.BlockSpec((1, LANES), const2),
        ],
        out_specs=(
            pl.BlockSpec((tm, D), lambda i: (i, 0)),
            pl.BlockSpec((tm, D // 2), lambda i: (i, 0)),
            pl.BlockSpec((tm, LANES), lambda i: (i, 0)),
            pl.BlockSpec((tm, LANES), lambda i: (i, 0)),
            pl.BlockSpec((SUBLANES, LANES), const2),
        ),
        compiler_params=_params(("arbitrary",), 56),
        name="merge",
    )(ya, yb, proj, proj, h, wa, wb, wo, g1.reshape(1, D), b1.reshape(1, D), wr_split, br)


def _dest_kernel(idx_ref, start_ref, dest_ref, carry):
    i = pl.program_id(0)
    tr = idx_ref.shape[0]

    @pl.when(i == 0)
    def _():
        carry[...] = jnp.zeros(carry.shape, F32)

    idx = idx_ref[...]
    lane = lax.broadcasted_iota(I32, (tr, LANES), 1)
    ohs = [(lane == idx[:, j:j + 1]).astype(F32) for j in range(TOP_K)]
    oh = ohs[0] + ohs[1] + ohs[2] + ohs[3]
    ri = lax.broadcasted_iota(I32, (tr, tr), 0)
    ci = lax.broadcasted_iota(I32, (tr, tr), 1)
    stril = (ri > ci).astype(BF16)
    before = jnp.dot(stril, oh.astype(BF16), preferred_element_type=F32) + carry[0:1, :] + start_ref[...]
    dest = jnp.zeros((tr, LANES), I32)
    for j in range(TOP_K):
        pos = jnp.sum(ohs[j] * before, axis=-1, keepdims=True)
        dest = jnp.where(lane == j, pos.astype(I32), dest)
    dest_ref[...] = dest
    carry[0:1, :] = carry[0:1, :] + jnp.sum(oh, axis=0, keepdims=True)


def _dest(idx, starts, tr):
    R = idx.shape[0]
    return pl.pallas_call(
        _dest_kernel,
        out_shape=jax.ShapeDtypeStruct((R, LANES), I32),
        grid=(R // tr,),
        in_specs=[pl.BlockSpec((tr, LANES), lambda i: (i, 0)), pl.BlockSpec((1, LANES), lambda i: (0, 0))],
        out_specs=pl.BlockSpec((tr, LANES), lambda i: (i, 0)),
        scratch_shapes=[pltpu.VMEM((SUBLANES, LANES), F32)],
        compiler_params=_params(("arbitrary",), 32),
        name="moe_dest",
    )(idx, starts)


ROW_UNROLL = 4


def _dispatch_kernel(nv_ref, dest_ref, x_ref, xb_ref, zbuf, sem, zsem):
    tm = x_ref.shape[0]
    bm = zbuf.shape[0]
    n_blocks = xb_ref.shape[0] // bm

    @pl.when(pl.program_id(0) == 0)
    def _():
        zbuf[...] = jnp.zeros(zbuf.shape, U32)

        def zero_copy(b):
            return pltpu.make_async_copy(zbuf, xb_ref.at[pl.ds(pl.multiple_of(b * bm, bm), bm), :], zsem)

        def start(b, c):
            @pl.when(nv_ref[b] < bm)
            def _():
                zero_copy(b).start()
            return c

        def wait(b, c):
            @pl.when(nv_ref[b] < bm)
            def _():
                zero_copy(b).wait()
            return c

        lax.fori_loop(0, n_blocks, start, 0)
        lax.fori_loop(0, n_blocks, wait, 0)

    def row(r, c):
        for j in range(TOP_K):
            d = dest_ref[r * TOP_K + j]
            pltpu.make_async_copy(x_ref.at[pl.ds(r, 1), :], xb_ref.at[pl.ds(d, 1), :], sem).start(priority=j % 2)
        return c

    lax.fori_loop(0, tm, row, 0, unroll=ROW_UNROLL)
    for j in range(TOP_K):
        pltpu.make_async_copy(x_ref, xb_ref.at[pl.ds(0, tm), :], sem).wait()


def _dispatch(n_valid, dest_flat, hp, m_pad, tm, bm):
    R, W = hp.shape
    grid_spec = pltpu.PrefetchScalarGridSpec(
        num_scalar_prefetch=1,
        grid=(R // tm,),
        in_specs=[
            pl.BlockSpec((tm * TOP_K,), lambda i, nv: (i,), memory_space=pltpu.SMEM),
            pl.BlockSpec((tm, W), lambda i, nv: (i, 0)),
        ],
        out_specs=pl.BlockSpec(memory_space=pl.ANY),
        scratch_shapes=[pltpu.VMEM((bm, W), U32), pltpu.SemaphoreType.DMA, pltpu.SemaphoreType.DMA],
    )
    return pl.pallas_call(
        _dispatch_kernel,
        out_shape=jax.ShapeDtypeStruct((m_pad, W), U32),
        grid_spec=grid_spec,
        compiler_params=_params(("arbitrary",), 32),
        name="moe_dispatch",
    )(n_valid, dest_flat, hp)


def _deint_kernel(w_ref, p_ref, g_ref, l_ref):
    w = w_ref[0].astype(BF16)
    perm = p_ref[...]
    width = perm.shape[0]
    half = width // 2
    for g in range(w.shape[1] // width):
        t = jnp.dot(w[:, g * width:(g + 1) * width], perm, preferred_element_type=F32)
        g_ref[0, :, g * half:(g + 1) * half] = t[:, :half].astype(BF16)
        l_ref[0, :, g * half:(g + 1) * half] = t[:, half:].astype(BF16)


def _deinterleave(w_gu, tr):
    N, D, DE2 = w_gu.shape
    DE = DE2 // 2
    width = 2 * LANES
    src = lax.broadcasted_iota(I32, (width, width), 0)
    dst = lax.broadcasted_iota(I32, (width, width), 1)
    perm = (src == jnp.where(dst < LANES, 2 * dst, 2 * (dst - LANES) + 1)).astype(BF16)
    out = jax.ShapeDtypeStruct((N, D, DE), BF16)
    return pl.pallas_call(
        _deint_kernel,
        out_shape=(out, out),
        grid=(N, D // tr),
        in_specs=[
            pl.BlockSpec((1, tr, DE2), lambda n, i: (n, i, 0)),
            pl.BlockSpec((width, width), lambda n, i: (0, 0)),
        ],
        out_specs=(pl.BlockSpec((1, tr, DE), lambda n, i: (n, i, 0)),
                   pl.BlockSpec((1, tr, DE), lambda n, i: (n, i, 0))),
        compiler_params=_params(("arbitrary", "arbitrary"), 40),
        name="deinterleave",
    )(w_gu, perm)


FFN_GATES = 4


def _ffn_kernel(be_ref, nv_ref, nu_ref, x_ref, wg_ref, bg_ref, wl_ref, bl_ref, wd32_ref, bd_ref, o_ref, wd_ref):
    b = pl.program_id(0)
    tm = x_ref.shape[0]
    q = tm // FFN_GATES
    used = b < nu_ref[0]
    gates_needed = jnp.maximum((nv_ref[b] + q - 1) // q, 1)

    @pl.when(jnp.logical_and(used, jnp.logical_or(b == 0, be_ref[b] != be_ref[jnp.maximum(b - 1, 0)])))
    def _():
        wd_ref[0] = wd32_ref[0].astype(BF16)

    def ffn_rows(rows):
        x = _unpack_pairs(x_ref[rows, :])
        glu = jnp.dot(x, wg_ref[0], preferred_element_type=F32) + bg_ref[0]
        lin = jnp.dot(x, wl_ref[0], preferred_element_type=F32) + bl_ref[0]
        glu = jnp.minimum(glu, SWIGLU_LIMIT)
        lin = jnp.clip(lin, -SWIGLU_LIMIT, SWIGLU_LIMIT)
        act = glu * _sigmoid(SWIGLU_ALPHA * glu) * (lin + 1.0)
        y = jnp.dot(act.astype(BF16), wd_ref[0], preferred_element_type=F32) + bd_ref[0]
        o_ref[rows, :] = _pack_pairs(y)

    @pl.when(jnp.logical_not(used))
    def _():
        o_ref[...] = jnp.zeros(o_ref.shape, U32)

    for k in range(1, FFN_GATES + 1):
        @pl.when(jnp.logical_and(used, gates_needed == k))
        def _(k=k):
            ffn_rows(slice(tm - k * q, tm))
            if k < FFN_GATES:
                o_ref[:tm - k * q, :] = jnp.zeros((tm - k * q, o_ref.shape[1]), U32)


def _ffn(block_exp, n_valid, n_used, xb, wg, bg, wl, bl, wd, bd, tm):
    m_pad, W = xb.shape
    _, D, DE = wg.shape
    n_blocks = m_pad // tm

    def rows(b, be, nv, nu):
        return (jnp.minimum(b, nu[0] - 1), 0)

    def expert(b, be, nv, nu):
        return (be[jnp.minimum(b, nu[0] - 1)], 0, 0)

    grid_spec = pltpu.PrefetchScalarGridSpec(
        num_scalar_prefetch=3,
        grid=(n_blocks,),
        in_specs=[
            pl.BlockSpec((tm, W), rows),
            pl.BlockSpec((1, D, DE), expert),
            pl.BlockSpec((1, 1, DE), expert),
            pl.BlockSpec((1, D, DE), expert),
            pl.BlockSpec((1, 1, DE), expert),
            pl.BlockSpec((1, DE, D), expert),
            pl.BlockSpec((1, 1, D), expert),
        ],
        out_specs=pl.BlockSpec((tm, W), lambda b, be, nv, nu: (b, 0)),
        scratch_shapes=[pltpu.VMEM((1, DE, D), BF16)],
    )
    return pl.pallas_call(
        _ffn_kernel,
        out_shape=jax.ShapeDtypeStruct((m_pad, W), U32),
        grid_spec=grid_spec,
        compiler_params=_params(("arbitrary",), 58),
        name="moe_ffn",
    )(block_exp, n_valid, n_used, xb, wg, bg, wl, bl, wd, bd)


def _combine_kernel(dest_ref, dnext_ref, gate_ref, h_ref, g2_ref, b2_ref, yb_ref, of_ref, ob_ref, buf, sems,
                    *, alpha):
    i = pl.program_id(0)
    tm, D = h_ref.shape
    slot = i % 2

    def gather(d_ref, s):
        def row(r, c):
            for j in range(TOP_K):
                d = d_ref[r * TOP_K + j]
                pltpu.make_async_copy(
                    yb_ref.at[pl.ds(d, 1), :], buf.at[s, j, pl.ds(r, 1), :], sems.at[s]).start(priority=j % 2)
            return c

        lax.fori_loop(0, tm, row, 0, unroll=ROW_UNROLL)

    @pl.when(i == 0)
    def _():
        gather(dest_ref, slot)

    @pl.when(i + 1 < pl.num_programs(0))
    def _():
        gather(dnext_ref, 1 - slot)

    for j in range(TOP_K):
        pltpu.make_async_copy(yb_ref.at[pl.ds(0, tm), :], buf.at[slot, j], sems.at[slot]).wait()

    gate = gate_ref[...]
    moe = None
    for j in range(TOP_K):
        term = gate[:, j:j + 1] * _unpack_pairs_f32(buf[slot, j])
        moe = term if moe is None else moe + term
    y = _layer_norm(alpha * h_ref[...] + moe, g2_ref[...], b2_ref[...])
    of_ref[...] = y
    ob_ref[...] = y.astype(BF16)


def _combine(dest_flat, gate, h, g2, b2, yb, alpha, tm):
    R, D = h.shape
    W = yb.shape[1]
    n = R // tm
    row = pl.BlockSpec((tm, D), lambda i: (i, 0))
    vec = pl.BlockSpec((1, D), lambda i: (0, 0))
    return pl.pallas_call(
        functools.partial(_combine_kernel, alpha=alpha),
        out_shape=(jax.ShapeDtypeStruct((R, D), F32), jax.ShapeDtypeStruct((R, D), BF16)),
        grid=(n,),
        in_specs=[
            pl.BlockSpec((tm * TOP_K,), lambda i: (i,), memory_space=pltpu.SMEM),
            pl.BlockSpec((tm * TOP_K,), lambda i: (jnp.minimum(i + 1, n - 1),), memory_space=pltpu.SMEM),
            pl.BlockSpec((tm, LANES), lambda i: (i, 0)),
            row, vec, vec,
            pl.BlockSpec(memory_space=pl.ANY),
        ],
        out_specs=(row, row),
        scratch_shapes=[pltpu.VMEM((2, TOP_K, tm, W), U32), pltpu.SemaphoreType.DMA((2,))],
        compiler_params=_params(("arbitrary",), 48),
        name="moe_combine",
    )(dest_flat, dest_flat, gate, h, g2.reshape(1, D), b2.reshape(1, D), yb)


def _lane_pad(v, offset):
    return jnp.zeros((1, LANES), F32).at[0, offset:offset + v.shape[0]].set(v.astype(F32))


def kernel(x, meta_tokens, ln_in_g, ln_in_b, w_in, rg_conv_w, rg_conv_b, rg_wx, rg_bx, rg_wa, rg_ba, rg_lambda, gdn_conv_w, gdn_a_log, gdn_dt_bias, gdn_norm_w, w_branch_a, w_branch_b, w_out, ln1_g, ln1_b, w_router, b_router, w_gate_up, b_gate_up, w_down, b_down, ln2_g, ln2_b):
    B, seq, D = x.shape
    depth = w_in.shape[0]
    E = w_router.shape[2]
    DE = w_down.shape[2]
    d_rnn = D // 2
    assert D == 2 * GDN_V_DIM and d_rnn == GDN_V_DIM and E <= LANES
    alpha = float((2 * depth) ** 0.25)

    T = N_META + seq
    pad = (-T) % CHUNK
    Tp = T + pad
    R = B * Tp

    tt_ln = _pick_tile(Tp, 832, CHUNK)
    tm_in = _pick_tile(R, 1040)
    tn_in = 1024
    tt_rg = _pick_tile(Tp, 416, SUBLANES)
    tt_gdn = _pick_tile(Tp, 320, CHUNK)
    tm_mg = _pick_tile(R, 256)
    tm_moe = _pick_tile(R, 256)
    bm_ffn = 2 * tm_moe
    n_blocks = -(-(R * TOP_K) // bm_ffn) + E
    m_pad = n_blocks * bm_ffn

    h, hb = _ln_in(x, meta_tokens, ln_in_g, ln_in_b, pad, tt_ln)

    wg_all, wl_all = _deinterleave(w_gate_up.reshape(depth * E, D, 2 * DE), _pick_tile(D, 512))
    wd_all = w_down.reshape(depth * E, DE, D)
    bg_all = b_gate_up.reshape(depth * E, 1, 2 * DE)[:, :, 0::2]
    bl_all = b_gate_up.reshape(depth * E, 1, 2 * DE)[:, :, 1::2]
    bd_all = b_down.reshape(depth * E, 1, D)

    o_rx = 0
    o_q = 2 * d_rnn
    o_z = o_q + 2 * GDN_QK_DIM + GDN_V_DIM
    o_beta = o_z + GDN_V_DIM
    o_ga = o_beta + 2 * GDN_H_V
    o_gb = o_ga + D
    block_start = jnp.arange(n_blocks, dtype=I32) * bm_ffn

    for l in range(depth):
        wl = w_in[l]
        w_main = jnp.concatenate(
            [wl[:, o_ga:o_ga + D], wl[:, o_gb:o_gb + D], wl[:, o_q:o_z], wl[:, o_rx:o_q], wl[:, o_z:o_beta]],
            axis=1).astype(BF16)
        w_small = jnp.zeros((D, LANES), F32).at[:, :2 * GDN_H_V].set(wl[:, o_beta:o_ga]).astype(BF16)
        proj, small = _inproj(hb, w_main, w_small, tm_in, tn_in)

        ya = _rglru(proj, rg_conv_w[l], rg_conv_b[l], rg_wx[l].astype(BF16), rg_bx[l], rg_wa[l].astype(BF16),
                    rg_ba[l], rg_lambda[l], B, Tp, pad, tt_rg, xcol=(2 * D + 2048) // d_rnn,
                    ycol=(2 * D + 2048) // d_rnn + 1)
        yb = _gdn(proj, small, gdn_conv_w[l], _lane_pad(gdn_a_log[l], GDN_H_V), _lane_pad(gdn_dt_bias[l], GDN_H_V),
                  gdn_norm_w[l].reshape(1, GDN_DV).astype(F32), B, Tp, pad, tt_gdn, qkvcol=2,
                  zcol=(2 * D + 2048 + 2 * d_rnn) // GDN_V_DIM)

        wr = jnp.zeros((D, LANES), F32).at[:, :E].set(w_router[l])
        br = jnp.full((1, LANES), -jnp.inf, F32).at[0, :E].set(b_router[l])
        hn, hp, idx, gate, counts = _merge(
            ya, yb, proj, h, w_branch_a[l].astype(BF16), w_branch_b[l].astype(BF16), w_out[l].astype(BF16),
            ln1_g[l], ln1_b[l], wr, br, alpha, tm_mg, gacol=0, gbcol=1)

        cnt = counts[0, :E].astype(I32)
        padded = (cnt + bm_ffn - 1) // bm_ffn * bm_ffn
        pad_ends = jnp.cumsum(padded)
        first_row = pad_ends - cnt
        block_exp = jnp.minimum(jnp.sum((pad_ends[None, :] <= block_start[:, None]).astype(I32), axis=1), E - 1)
        onehot = (block_exp[:, None] == jnp.arange(E, dtype=I32)[None, :]).astype(I32)
        n_valid = jnp.clip(block_start + bm_ffn - jnp.sum(onehot * first_row[None, :], axis=1), 0, bm_ffn)
        n_valid = jnp.where(block_start < pad_ends[E - 1], n_valid, 0).astype(I32)
        n_used = (pad_ends[E - 1:E] // bm_ffn).astype(I32)

        dest = _dest(idx, _lane_pad(first_row, 0), tm_moe)
        dest_flat = dest[:, :TOP_K].reshape(R * TOP_K)

        xb = _dispatch(n_valid, dest_flat, hp, m_pad, tm_moe, bm_ffn)
        ybk = _ffn(block_exp + l * E, n_valid, n_used, xb, wg_all, bg_all, wl_all, bl_all, wd_all, bd_all, bm_ffn)
        h, hb = _combine(dest_flat, gate, hn, ln2_g[l], ln2_b[l], ybk, alpha, tm_moe)

    return h.reshape(B, Tp, D)[:, pad + N_META:]
```

```python
import functools

import jax
import jax.numpy as jnp
from jax import lax
from jax.experimental import pallas as pl
from jax.experimental.pallas import tpu as pltpu

F32 = jnp.float32
BF16 = jnp.bfloat16
I32 = jnp.int32
U32 = jnp.uint32

N_META = 16
CONV_W = 4
CHUNK = 64
TOP_K = 4
RG_BLOCKS = 8
RG_C = 8.0
GDN_DK = 128
GDN_DV = 128
GDN_H_QK = 4
GDN_H_V = 8
GDN_QK_DIM = GDN_H_QK * GDN_DK
GDN_V_DIM = GDN_H_V * GDN_DV
SWIGLU_LIMIT = 7.0
SWIGLU_ALPHA = 1.702
LN_EPS = 1e-5
RMS_EPS = 1e-6

LANES = 128
SUBLANES = 8
MIB = 1024 * 1024
HIGHEST = lax.Precision.HIGHEST


def _pick_tile(n, target, mult=16):
    best = None
    for t in range(mult, min(n, target) + 1, mult):
        if n % t == 0:
            best = t
    assert best is not None, (n, target, mult)
    return best


def _params(sem, vmem_mib):
    return pltpu.CompilerParams(dimension_semantics=sem, vmem_limit_bytes=vmem_mib * MIB)


def _layer_norm(x, g, b):
    mu = jnp.mean(x, axis=-1, keepdims=True)
    xc = x - mu
    var = jnp.mean(xc * xc, axis=-1, keepdims=True)
    return xc * lax.rsqrt(var + LN_EPS) * g + b


def _sigmoid(x):
    return 1.0 / (1.0 + jnp.exp(-x))


def _softplus(x):
    return jnp.maximum(x, 0.0) + jnp.log1p(jnp.exp(-jnp.abs(x)))


def _silu(x):
    return x * _sigmoid(x)


def _gelu_tanh(x):
    return 0.5 * x * (1.0 + jnp.tanh(0.7978845608028654 * (x + 0.044715 * x * x * x)))


def _ln_in_kernel(x_ref, meta_ref, g_ref, b_ref, of_ref, ob_ref, xin, sem, *, pad):
    bi = pl.program_id(0)
    t = pl.program_id(1)
    tt = xin.shape[0]
    lead = pad + meta_ref.shape[0]

    @pl.when(t == 0)
    def _():
        if pad:
            xin[0:pad, :] = jnp.zeros((pad, xin.shape[1]), F32)
        xin[pad:lead, :] = meta_ref[...]
        if tt > lead:
            cp = pltpu.make_async_copy(
                x_ref.at[bi, pl.ds(0, tt - lead), :], xin.at[pl.ds(lead, tt - lead), :], sem)
            cp.start()
            cp.wait()

    @pl.when(t > 0)
    def _():
        start = pl.multiple_of(t * tt - lead, SUBLANES)
        cp = pltpu.make_async_copy(x_ref.at[bi, pl.ds(start, tt), :], xin, sem)
        cp.start()
        cp.wait()

    y = _layer_norm(xin[...], g_ref[...], b_ref[...])
    of_ref[...] = y
    ob_ref[...] = y.astype(BF16)


def _ln_in(x, meta, g, b, pad, tt):
    B, seq, D = x.shape
    lead = pad + meta.shape[0]
    Tp = seq + lead
    nT = Tp // tt
    assert lead % SUBLANES == 0 and lead <= tt and Tp % tt == 0
    row = pl.BlockSpec((tt, D), lambda bi, t: (bi * nT + t, 0))
    vec = pl.BlockSpec((1, D), lambda bi, t: (0, 0))
    return pl.pallas_call(
        functools.partial(_ln_in_kernel, pad=pad),
        out_shape=(jax.ShapeDtypeStruct((B * Tp, D), F32), jax.ShapeDtypeStruct((B * Tp, D), BF16)),
        grid=(B, nT),
        in_specs=[pl.BlockSpec(memory_space=pl.ANY), pl.BlockSpec(meta.shape, lambda bi, t: (0, 0)), vec, vec],
        out_specs=(row, row),
        scratch_shapes=[pltpu.VMEM((tt, D), F32), pltpu.SemaphoreType.DMA],
        compiler_params=_params(("arbitrary", "arbitrary"), 48),
        name="ln_in",
    )(x, meta.astype(F32), g.reshape(1, D), b.reshape(1, D))


def _inproj_kernel(x_ref, w_ref, ws_ref, o_ref, os_ref):
    x = x_ref[...]
    o_ref[...] = jnp.dot(x, w_ref[...], preferred_element_type=F32)

    @pl.when(pl.program_id(1) == 0)
    def _():
        os_ref[...] = jnp.dot(x, ws_ref[...], preferred_element_type=F32)


def _inproj(hb, w_main, w_small, tm, tn):
    R, D = hb.shape
    N = w_main.shape[1]
    return pl.pallas_call(
        _inproj_kernel,
        out_shape=(jax.ShapeDtypeStruct((R, N), F32), jax.ShapeDtypeStruct((R, LANES), F32)),
        grid=(R // tm, N // tn),
        in_specs=[
            pl.BlockSpec((tm, D), lambda i, j: (i, 0)),
            pl.BlockSpec((D, tn), lambda i, j: (0, j)),
            pl.BlockSpec((D, LANES), lambda i, j: (0, 0)),
        ],
        out_specs=(
            pl.BlockSpec((tm, tn), lambda i, j: (i, j)),
            pl.BlockSpec((tm, LANES), lambda i, j: (i, 0)),
        ),
        compiler_params=_params(("arbitrary", "arbitrary"), 48),
        name="inproj",
    )(hb, w_main, w_small)


def _causal_conv(x, xbuf, cw_ref, tt):
    xbuf[SUBLANES:SUBLANES + tt, :] = x
    y = cw_ref[CONV_W - 1:CONV_W, :] * x
    for j in range(CONV_W - 1):
        off = SUBLANES - (CONV_W - 1) + j
        y = y + cw_ref[j:j + 1, :] * xbuf[off:off + tt, :]
    xbuf[0:SUBLANES, :] = xbuf[tt:tt + SUBLANES, :]
    return y


def _rglru_kernel(x_ref, y_ref, cw_ref, cb_ref, wx_ref, bx_ref, wa_ref, ba_ref, lam_ref,
                  o_ref, xbuf, a_s, u_s, hc, *, pad):
    t = pl.program_id(1)
    tt, C = x_ref.shape
    bs = C // RG_BLOCKS

    @pl.when(t == 0)
    def _():
        xbuf[0:SUBLANES, :] = jnp.zeros((SUBLANES, C), F32)
        hc[...] = jnp.zeros((SUBLANES, C), F32)

    row = lax.broadcasted_iota(I32, (tt, 1), 0)
    valid = (row + t * tt) >= pad
    x = jnp.where(valid, x_ref[...], 0.0)
    xc = _causal_conv(x, xbuf, cw_ref, tt) + cb_ref[...]

    xcb = xc.astype(BF16)
    gi = jnp.concatenate(
        [jnp.dot(xcb[:, h * bs:(h + 1) * bs], wx_ref[h], preferred_element_type=F32) for h in range(RG_BLOCKS)],
        axis=1)
    gr = jnp.concatenate(
        [jnp.dot(xcb[:, h * bs:(h + 1) * bs], wa_ref[h], preferred_element_type=F32) for h in range(RG_BLOCKS)],
        axis=1)
    gi = _sigmoid(gi + bx_ref[...])
    gr = _sigmoid(gr + ba_ref[...])
    log_a = (-RG_C) * gr * _softplus(-lam_ref[...])
    a = jnp.exp(log_a)
    u = jnp.sqrt(-jnp.tanh(log_a) * (a * a + 1.0)) * gi * xc
    u = jnp.where(valid, u, 0.0)

    groups = tt // SUBLANES
    a = a.reshape(groups, SUBLANES, C)
    u = u.reshape(groups, SUBLANES, C)
    rowmod = lax.broadcasted_iota(I32, (groups, SUBLANES, 1), 1)
    for d in (1, 2, 4):
        m = rowmod >= d
        a_sh = pltpu.roll(a, d, 1)
        u_sh = pltpu.roll(u, d, 1)
        u = jnp.where(m, a * u_sh + u, u)
        a = jnp.where(m, a * a_sh, a)
    a_s[...] = a.reshape(tt, C)
    u_s[...] = u.reshape(tt, C)

    def group(g, hprev):
        r0 = pl.multiple_of(g * SUBLANES, SUBLANES)
        h = a_s[pl.ds(r0, SUBLANES), :] * hprev + u_s[pl.ds(r0, SUBLANES), :]
        u_s[pl.ds(r0, SUBLANES), :] = h
        return jnp.broadcast_to(h[SUBLANES - 1:SUBLANES, :], (SUBLANES, C))

    hc[...] = lax.fori_loop(0, tt // SUBLANES, group, hc[...])
    o_ref[...] = (u_s[...] * _gelu_tanh(y_ref[...])).astype(BF16)


def _rglru(proj, cw, cb, wx, bx, wa, ba, lam, B, Tp, pad, tt, xcol, ycol):
    R = proj.shape[0]
    C = cw.shape[1]
    nT = Tp // tt
    vec = pl.BlockSpec((1, C), lambda b, t: (0, 0))
    wspec = pl.BlockSpec(wx.shape, lambda b, t: (0, 0, 0))
    return pl.pallas_call(
        functools.partial(_rglru_kernel, pad=pad),
        out_shape=jax.ShapeDtypeStruct((R, C), BF16),
        grid=(B, nT),
        in_specs=[
            pl.BlockSpec((tt, C), lambda b, t: (b * nT + t, xcol)),
            pl.BlockSpec((tt, C), lambda b, t: (b * nT + t, ycol)),
            pl.BlockSpec((CONV_W, C), lambda b, t: (0, 0)),
            vec, wspec, vec, wspec, vec, vec,
        ],
        out_specs=pl.BlockSpec((tt, C), lambda b, t: (b * nT + t, 0)),
        scratch_shapes=[
            pltpu.VMEM((tt + SUBLANES, C), F32),
            pltpu.VMEM((tt, C), F32),
            pltpu.VMEM((tt, C), F32),
            pltpu.VMEM((SUBLANES, C), F32),
        ],
        compiler_params=_params(("arbitrary", "arbitrary"), 48),
        name="rglru",
    )(proj, proj, cw, cb.reshape(1, C), wx, bx.reshape(1, C), wa, ba.reshape(1, C), lam.reshape(1, C))


def _bdot(a, b):
    return jnp.dot(a.astype(BF16), b.astype(BF16), preferred_element_type=F32)


def _bdot_nt(a, b):
    return lax.dot_general(a.astype(BF16), b.astype(BF16), (((1,), (1,)), ((), ())), preferred_element_type=F32)


def _bdot_tn(a, b):
    return lax.dot_general(a.astype(BF16), b.astype(BF16), (((0,), (0,)), ((), ())), preferred_element_type=F32)


def _unit_lower_inverses(Ls, masks):
    eye = masks[0]
    dinvs = [eye - L * masks[1] for L in Ls]
    for m in masks[2:]:
        ts = [_bdot(d, L * m) for d, L in zip(dinvs, Ls)]
        dinvs = [d - _bdot(t, d) for d, t in zip(dinvs, ts)]
    return dinvs


def _gdn_kernel(qkv_ref, z_ref, sm_ref, cw_ref, alog_ref, dtb_ref, nw_ref,
                o_ref, xbuf, q_s, k_s, v_s, b_s, g_s, S_ref, val_s, kq_s, at_s, kd_s, eg_s, *, pad):
    t = pl.program_id(1)
    tt, CQ = qkv_ref.shape
    ncht = tt // CHUNK
    rep = GDN_H_V // GDN_H_QK

    @pl.when(t == 0)
    def _():
        xbuf[0:SUBLANES, :] = jnp.zeros((SUBLANES, CQ), F32)
        S_ref[...] = jnp.zeros(S_ref.shape, F32)

    row = lax.broadcasted_iota(I32, (tt, 1), 0)
    valid = (row + t * tt) >= pad
    x = jnp.where(valid, qkv_ref[...], 0.0)
    xc = _silu(_causal_conv(x, xbuf, cw_ref, tt))

    for hq in range(GDN_H_QK):
        qh = xc[:, hq * GDN_DK:(hq + 1) * GDN_DK]
        kh = xc[:, GDN_QK_DIM + hq * GDN_DK:GDN_QK_DIM + (hq + 1) * GDN_DK]
        qn = qh * lax.rsqrt(jnp.sum(qh * qh, axis=-1, keepdims=True) + RMS_EPS) * (GDN_DK ** -0.5)
        kn = kh * lax.rsqrt(jnp.sum(kh * kh, axis=-1, keepdims=True) + RMS_EPS)
        q_s[:, hq * GDN_DK:(hq + 1) * GDN_DK] = qn
        k_s[:, hq * GDN_DK:(hq + 1) * GDN_DK] = kn
    v_s[...] = xc[:, 2 * GDN_QK_DIM:]

    sm = sm_ref[...]
    b_s[...] = _sigmoid(sm)
    g = -jnp.exp(alog_ref[...]) * _softplus(sm + dtb_ref[...])
    g_s[...] = jnp.where(valid, g, 0.0)

    ri = lax.broadcasted_iota(I32, (CHUNK, CHUNK), 0)
    ci = lax.broadcasted_iota(I32, (CHUNK, CHUNK), 1)
    causal = ri >= ci
    strict = ri > ci
    tril = causal.astype(F32)
    masks = [(ri == ci).astype(F32)]
    s = 1
    while s < CHUNK:
        masks.append(((ri // (2 * s) == ci // (2 * s)) & (ri % (2 * s) >= s) & (ci % (2 * s) < s)).astype(F32))
        s *= 2
    nw = nw_ref[...]

    heads = range(GDN_H_V)
    chunks = range(ncht)
    inst = [(c, h) for c in chunks for h in heads]

    rows = [slice(c * CHUNK, (c + 1) * CHUNK) for c in chunks]
    G = [jnp.dot(tril, g_s[rows[c], :], precision=HIGHEST, preferred_element_type=F32) for c in chunks]
    GT = [G[c].T for c in chunks]
    bet = [b_s[rows[c], :] for c in chunks]
    qs = [[q_s[rows[c], hq * GDN_DK:(hq + 1) * GDN_DK] for hq in range(GDN_H_QK)] for c in chunks]
    ks_ = [[k_s[rows[c], hq * GDN_DK:(hq + 1) * GDN_DK] for hq in range(GDN_H_QK)] for c in chunks]
    kk = [[_bdot_nt(k, k) for k in ks_[c]] for c in chunks]
    qk = [[_bdot_nt(q, k) for q, k in zip(qs[c], ks_[c])] for c in chunks]
    bcol = {(c, h): bet[c][:, h:h + 1] for c, h in inst}
    Gc = {(c, h): G[c][:, GDN_H_V + h:GDN_H_V + h + 1] for c, h in inst}
    decay = {(c, h): jnp.exp(jnp.where(
        causal, Gc[c, h] - GT[c][GDN_H_V + h:GDN_H_V + h + 1, :], -jnp.inf)) for c, h in inst}
    eG = {i: jnp.exp(Gc[i]) for i in inst}
    g_last = {i: Gc[i][CHUNK - 1:CHUNK, :] for i in inst}
    Ls = [jnp.where(strict, kk[c][h // rep] * bcol[c, h] * decay[c, h], 0.0) for c, h in inst]
    Ts = dict(zip(inst, _unit_lower_inverses(Ls, masks)))
    for c, h in inst:
        k = ks_[c][h // rep]
        sol = _bdot(Ts[c, h], jnp.concatenate(
            [v_s[rows[c], h * GDN_DV:(h + 1) * GDN_DV] * bcol[c, h], k * (bcol[c, h] * eG[c, h])], axis=1))
        val_s[c, h] = sol[:, :GDN_DV]
        kq_s[c, h] = jnp.concatenate([sol[:, GDN_DV:], qs[c][h // rep] * eG[c, h]], axis=0).astype(BF16)
        at_s[c, h] = (qk[c][h // rep] * decay[c, h]).astype(BF16)
        kd_s[c, h] = (k * jnp.exp(g_last[c, h] - Gc[c, h])).astype(BF16)
        eg_s[c, h] = jnp.broadcast_to(jnp.exp(g_last[c, h]), (SUBLANES, GDN_DV))

    for c in chunks:
        zc = z_ref[rows[c], :]
        Sold = [S_ref[h] for h in heads]
        PS = [_bdot(kq_s[c, h], Sold[h]) for h in heads]
        v_new = [val_s[c, h] - PS[h][:CHUNK] for h in heads]
        o = [PS[h][CHUNK:] + _bdot(at_s[c, h], v_new[h]) for h in heads]
        for h in heads:
            S_ref[h] = Sold[h] * eg_s[c, h][0:1, :] + _bdot_tn(kd_s[c, h], v_new[h])
        for h in heads:
            oh = o[h] * lax.rsqrt(jnp.mean(o[h] * o[h], axis=-1, keepdims=True) + RMS_EPS) * nw
            oh = oh * _silu(zc[:, h * GDN_DV:(h + 1) * GDN_DV])
            o_ref[rows[c], h * GDN_DV:(h + 1) * GDN_DV] = oh.astype(BF16)


def _gdn(proj, small, cw, alog_pad, dtb_pad, nw, B, Tp, pad, tt, qkvcol, zcol):
    R = proj.shape[0]
    CQ = cw.shape[1]
    nT = Tp // tt
    lane_vec = pl.BlockSpec((1, LANES), lambda b, t: (0, 0))
    return pl.pallas_call(
        functools.partial(_gdn_kernel, pad=pad),
        out_shape=jax.ShapeDtypeStruct((R, GDN_V_DIM), BF16),
        grid=(B, nT),
        in_specs=[
            pl.BlockSpec((tt, CQ), lambda b, t: (b * nT + t, qkvcol)),
            pl.BlockSpec((tt, GDN_V_DIM), lambda b, t: (b * nT + t, zcol)),
            pl.BlockSpec((tt, LANES), lambda b, t: (b * nT + t, 0)),
            pl.BlockSpec((CONV_W, CQ), lambda b, t: (0, 0)),
            lane_vec, lane_vec, lane_vec,
        ],
        out_specs=pl.BlockSpec((tt, GDN_V_DIM), lambda b, t: (b * nT + t, 0)),
        scratch_shapes=[
            pltpu.VMEM((tt + SUBLANES, CQ), F32),
            pltpu.VMEM((tt, GDN_QK_DIM), F32),
            pltpu.VMEM((tt, GDN_QK_DIM), F32),
            pltpu.VMEM((tt, GDN_V_DIM), F32),
            pltpu.VMEM((tt, LANES), F32),
            pltpu.VMEM((tt, LANES), F32),
            pltpu.VMEM((GDN_H_V, GDN_DK, GDN_DV), F32),
            pltpu.VMEM((tt // CHUNK, GDN_H_V, CHUNK, GDN_DV), F32),
            pltpu.VMEM((tt // CHUNK, GDN_H_V, 2 * CHUNK, GDN_DK), BF16),
            pltpu.VMEM((tt // CHUNK, GDN_H_V, CHUNK, CHUNK), BF16),
            pltpu.VMEM((tt // CHUNK, GDN_H_V, CHUNK, GDN_DK), BF16),
            pltpu.VMEM((tt // CHUNK, GDN_H_V, SUBLANES, GDN_DV), F32),
        ],
        compiler_params=_params(("arbitrary", "arbitrary"), 48),
        name="gdn",
    )(proj, proj, small, cw, alog_pad, dtb_pad, nw)


MERGE_SUBTILES = 2


def _merge_kernel(ya_ref, yb_ref, ga_ref, gb_ref, h_ref, wa_ref, wb_ref, wo_ref, g1_ref, b1_ref,
                  wr_ref, br_ref, hf_ref, idx_ref, gate_ref, cnt_ref, *, alpha):
    i = pl.program_id(0)
    tm, D = h_ref.shape
    sub = tm // MERGE_SUBTILES
    lane = lax.broadcasted_iota(I32, (sub, LANES), 1)
    counts = jnp.zeros((1, LANES), F32)
    subs = range(MERGE_SUBTILES)
    rows_of = [slice(s * sub, (s + 1) * sub) for s in subs]
    branch = [(jnp.dot(ya_ref[r, :], wa_ref[...], preferred_element_type=F32),
               jnp.dot(yb_ref[r, :], wb_ref[...], preferred_element_type=F32)) for r in rows_of]
    outs = []
    for s in subs:
        r = rows_of[s]
        mixed = _sigmoid(ga_ref[r, :]) * branch[s][0] + _sigmoid(gb_ref[r, :]) * branch[s][1]
        outs.append(jnp.dot(mixed.astype(BF16), wo_ref[...], preferred_element_type=F32))
    logits = []
    for s in subs:
        r = rows_of[s]
        hn = _layer_norm(alpha * h_ref[r, :] + outs[s], g1_ref[...], b1_ref[...])
        hf_ref[r, :] = hn
        hn_hi = hn.astype(BF16)
        hn_lo = (hn - hn_hi.astype(F32)).astype(BF16)
        p_hi = jnp.dot(hn_hi, wr_ref[...], preferred_element_type=F32)
        p_lo = jnp.dot(hn_lo, wr_ref[:, :LANES], preferred_element_type=F32)
        logits.append(p_hi[:, :LANES] + p_hi[:, LANES:] + p_lo + br_ref[...])
    vals = list(logits)
    tops = [[] for _ in subs]
    sels = [[] for _ in subs]
    idx_outs = [jnp.zeros((sub, LANES), I32) for _ in subs]
    for j in range(TOP_K):
        ms = [jnp.max(vals[s], axis=-1, keepdims=True) for s in subs]
        idxs = [jnp.min(jnp.where(vals[s] == ms[s], lane, LANES), axis=-1, keepdims=True) for s in subs]
        for s in subs:
            sel = lane == idxs[s]
            tops[s].append(ms[s])
            sels[s].append(sel)
            idx_outs[s] = jnp.where(lane == j, idxs[s], idx_outs[s])
            vals[s] = jnp.where(sel, -jnp.inf, vals[s])
    for s in subs:
        rows = rows_of[s]
        idx_out = idx_outs[s]
        es = [jnp.exp(m - tops[s][0]) for m in tops[s]]
        inv = 1.0 / (es[0] + es[1] + es[2] + es[3])
        gate_out = jnp.zeros((sub, LANES), F32)
        onehot = jnp.zeros((sub, LANES), F32)
        for j in range(TOP_K):
            gate_out = jnp.where(lane == j, es[j] * inv, gate_out)
            onehot = onehot + sels[s][j].astype(F32)
        idx_ref[rows, :] = idx_out
        gate_ref[rows, :] = gate_out
        counts = counts + jnp.sum(onehot, axis=0, keepdims=True)

    @pl.when(i == 0)
    def _():
        cnt_ref[...] = jnp.zeros(cnt_ref.shape, F32)

    cnt_ref[0:1, :] = cnt_ref[0:1, :] + counts


def _merge(ya, yb, proj, h, wa, wb, wo, g1, b1, wr, br, alpha, tm, gacol, gbcol):
    R, D = h.shape
    C = ya.shape[1]
    const2 = lambda i: (0, 0)
    single = pl.Buffered(1)
    wr_hi = wr.astype(BF16)
    wr_split = jnp.concatenate([wr_hi, (wr - wr_hi.astype(F32)).astype(BF16)], axis=1)
    return pl.pallas_call(
        functools.partial(_merge_kernel, alpha=alpha),
        out_shape=(
            jax.ShapeDtypeStruct((R, D), F32),
            jax.ShapeDtypeStruct((R, LANES), I32),
            jax.ShapeDtypeStruct((R, LANES), F32),
            jax.ShapeDtypeStruct((SUBLANES, LANES), F32),
        ),
        grid=(R // tm,),
        in_specs=[
            pl.BlockSpec((tm, C), lambda i: (i, 0)),
            pl.BlockSpec((tm, C), lambda i: (i, 0)),
            pl.BlockSpec((tm, D), lambda i: (i, gacol)),
            pl.BlockSpec((tm, D), lambda i: (i, gbcol)),
            pl.BlockSpec((tm, D), lambda i: (i, 0)),
            pl.BlockSpec((C, D), const2, pipeline_mode=single),
            pl.BlockSpec((C, D), const2, pipeline_mode=single),
            pl.BlockSpec((D, D), const2, pipeline_mode=single),
            pl.BlockSpec((1, D), const2),
            pl.BlockSpec((1, D), const2),
            pl.BlockSpec((D, 2 * LANES), const2, pipeline_mode=single),
            pl.BlockSpec((1, LANES), const2),
        ],
        out_specs=(
            pl.BlockSpec((tm, D), lambda i: (i, 0)),
            pl.BlockSpec((tm, LANES), lambda i: (i, 0)),
            pl.BlockSpec((tm, LANES), lambda i: (i, 0)),
            pl.BlockSpec((SUBLANES, LANES), const2),
        ),
        compiler_params=_params(("arbitrary",), 56),
        name="merge",
    )(ya, yb, proj, proj, h, wa, wb, wo, g1.reshape(1, D), b1.reshape(1, D), wr_split, br)


def _dest_kernel(idx_ref, start_ref, dest_ref, carry):
    i = pl.program_id(0)
    tr = idx_ref.shape[0]

    @pl.when(i == 0)
    def _():
        carry[...] = jnp.zeros(carry.shape, F32)

    idx = idx_ref[...]
    lane = lax.broadcasted_iota(I32, (tr, LANES), 1)
    ohs = [(lane == idx[:, j:j + 1]).astype(F32) for j in range(TOP_K)]
    oh = ohs[0] + ohs[1] + ohs[2] + ohs[3]
    ri = lax.broadcasted_iota(I32, (tr, tr), 0)
    ci = lax.broadcasted_iota(I32, (tr, tr), 1)
    stril = (ri > ci).astype(BF16)
    before = jnp.dot(stril, oh.astype(BF16), preferred_element_type=F32) + carry[0:1, :] + start_ref[...]
    dest = jnp.zeros((tr, LANES), I32)
    for j in range(TOP_K):
        pos = jnp.sum(ohs[j] * before, axis=-1, keepdims=True)
        dest = jnp.where(lane == j, pos.astype(I32), dest)
    dest_ref[...] = dest
    carry[0:1, :] = carry[0:1, :] + jnp.sum(oh, axis=0, keepdims=True)


def _dest(idx, starts, tr):
    R = idx.shape[0]
    return pl.pallas_call(
        _dest_kernel,
        out_shape=jax.ShapeDtypeStruct((R, LANES), I32),
        grid=(R // tr,),
        in_specs=[pl.BlockSpec((tr, LANES), lambda i: (i, 0)), pl.BlockSpec((1, LANES), lambda i: (0, 0))],
        out_specs=pl.BlockSpec((tr, LANES), lambda i: (i, 0)),
        scratch_shapes=[pltpu.VMEM((SUBLANES, LANES), F32)],
        compiler_params=_params(("arbitrary",), 32),
        name="moe_dest",
    )(idx, starts)


ROW_UNROLL = 8


def _dispatch_kernel(nv_ref, dest_ref, x_ref, xb_ref, zbuf, sem, zsem):
    tm = x_ref.shape[0]
    bm = zbuf.shape[0]
    n_blocks = xb_ref.shape[0] // bm

    @pl.when(pl.program_id(0) == 0)
    def _():
        zbuf[...] = jnp.zeros(zbuf.shape, F32)

        def zero_copy(b):
            return pltpu.make_async_copy(zbuf, xb_ref.at[pl.ds(pl.multiple_of(b * bm, bm), bm), :], zsem)

        def start(b, c):
            @pl.when(nv_ref[b] < bm)
            def _():
                zero_copy(b).start()
            return c

        def wait(b, c):
            @pl.when(nv_ref[b] < bm)
            def _():
                zero_copy(b).wait()
            return c

        lax.fori_loop(0, n_blocks, start, 0)
        lax.fori_loop(0, n_blocks, wait, 0)

    def row(r, c):
        for j in range(TOP_K):
            d = dest_ref[r * TOP_K + j]
            pltpu.make_async_copy(x_ref.at[pl.ds(r, 1), :], xb_ref.at[pl.ds(d, 1), :], sem).start(priority=j % 2)
        return c

    lax.fori_loop(0, tm, row, 0, unroll=ROW_UNROLL)
    for j in range(TOP_K):
        pltpu.make_async_copy(x_ref, xb_ref.at[pl.ds(0, tm), :], sem).wait()


def _dispatch(n_valid, dest_flat, hp, m_pad, tm, bm):
    R, W = hp.shape
    grid_spec = pltpu.PrefetchScalarGridSpec(
        num_scalar_prefetch=1,
        grid=(R // tm,),
        in_specs=[
            pl.BlockSpec((tm * TOP_K,), lambda i, nv: (i,), memory_space=pltpu.SMEM),
            pl.BlockSpec((tm, W), lambda i, nv: (i, 0)),
        ],
        out_specs=pl.BlockSpec(memory_space=pl.ANY),
        scratch_shapes=[pltpu.VMEM((bm, W), F32), pltpu.SemaphoreType.DMA, pltpu.SemaphoreType.DMA],
    )
    return pl.pallas_call(
        _dispatch_kernel,
        out_shape=jax.ShapeDtypeStruct((m_pad, W), F32),
        grid_spec=grid_spec,
        compiler_params=_params(("arbitrary",), 32),
        name="moe_dispatch",
    )(n_valid, dest_flat, hp)


def _deint_kernel(w_ref, p_ref, g_ref, l_ref):
    w = w_ref[0].astype(BF16)
    perm = p_ref[...]
    width = perm.shape[0]
    half = width // 2
    for g in range(w.shape[1] // width):
        t = jnp.dot(w[:, g * width:(g + 1) * width], perm, preferred_element_type=F32)
        g_ref[0, :, g * half:(g + 1) * half] = t[:, :half].astype(BF16)
        l_ref[0, :, g * half:(g + 1) * half] = t[:, half:].astype(BF16)


def _deinterleave(w_gu, tr):
    N, D, DE2 = w_gu.shape
    DE = DE2 // 2
    width = 2 * LANES
    src = lax.broadcasted_iota(I32, (width, width), 0)
    dst = lax.broadcasted_iota(I32, (width, width), 1)
    perm = (src == jnp.where(dst < LANES, 2 * dst, 2 * (dst - LANES) + 1)).astype(BF16)
    out = jax.ShapeDtypeStruct((N, D, DE), BF16)
    return pl.pallas_call(
        _deint_kernel,
        out_shape=(out, out),
        grid=(N, D // tr),
        in_specs=[
            pl.BlockSpec((1, tr, DE2), lambda n, i: (n, i, 0)),
            pl.BlockSpec((width, width), lambda n, i: (0, 0)),
        ],
        out_specs=(pl.BlockSpec((1, tr, DE), lambda n, i: (n, i, 0)),
                   pl.BlockSpec((1, tr, DE), lambda n, i: (n, i, 0))),
        compiler_params=_params(("arbitrary", "arbitrary"), 40),
        name="deinterleave",
    )(w_gu, perm)


FFN_GATES = 4


def _ffn_kernel(be_ref, nv_ref, nu_ref, x_ref, wg_ref, bg_ref, wl_ref, bl_ref, wd32_ref, bd_ref, o_ref, wd_ref):
    b = pl.program_id(0)
    tm = x_ref.shape[0]
    q = tm // FFN_GATES
    used = b < nu_ref[0]
    gates_needed = jnp.maximum((nv_ref[b] + q - 1) // q, 1)

    @pl.when(jnp.logical_and(used, jnp.logical_or(b == 0, be_ref[b] != be_ref[jnp.maximum(b - 1, 0)])))
    def _():
        wd_ref[0] = wd32_ref[0].astype(BF16)

    def ffn_rows(rows):
        x = x_ref[rows, :].astype(BF16)
        glu = jnp.dot(x, wg_ref[0], preferred_element_type=F32) + bg_ref[0]
        lin = jnp.dot(x, wl_ref[0], preferred_element_type=F32) + bl_ref[0]
        glu = jnp.minimum(glu, SWIGLU_LIMIT)
        lin = jnp.clip(lin, -SWIGLU_LIMIT, SWIGLU_LIMIT)
        act = glu * _sigmoid(SWIGLU_ALPHA * glu) * (lin + 1.0)
        y = jnp.dot(act.astype(BF16), wd_ref[0], preferred_element_type=F32) + bd_ref[0]
        o_ref[rows, :] = y

    @pl.when(jnp.logical_not(used))
    def _():
        o_ref[...] = jnp.zeros(o_ref.shape, F32)

    for k in range(1, FFN_GATES + 1):
        @pl.when(jnp.logical_and(used, gates_needed == k))
        def _(k=k):
            ffn_rows(slice(tm - k * q, tm))
            if k < FFN_GATES:
                o_ref[:tm - k * q, :] = jnp.zeros((tm - k * q, o_ref.shape[1]), F32)


def _ffn(block_exp, n_valid, n_used, xb, wg, bg, wl, bl, wd, bd, tm):
    m_pad, W = xb.shape
    _, D, DE = wg.shape
    n_blocks = m_pad // tm

    def rows(b, be, nv, nu):
        return (jnp.minimum(b, nu[0] - 1), 0)

    def expert(b, be, nv, nu):
        return (be[jnp.minimum(b, nu[0] - 1)], 0, 0)

    grid_spec = pltpu.PrefetchScalarGridSpec(
        num_scalar_prefetch=3,
        grid=(n_blocks,),
        in_specs=[
            pl.BlockSpec((tm, W), rows),
            pl.BlockSpec((1, D, DE), expert),
            pl.BlockSpec((1, 1, DE), expert),
            pl.BlockSpec((1, D, DE), expert),
            pl.BlockSpec((1, 1, DE), expert),
            pl.BlockSpec((1, DE, D), expert),
            pl.BlockSpec((1, 1, D), expert),
        ],
        out_specs=pl.BlockSpec((tm, W), lambda b, be, nv, nu: (b, 0)),
        scratch_shapes=[pltpu.VMEM((1, DE, D), BF16)],
    )
    return pl.pallas_call(
        _ffn_kernel,
        out_shape=jax.ShapeDtypeStruct((m_pad, W), F32),
        grid_spec=grid_spec,
        compiler_params=_params(("arbitrary",), 58),
        name="moe_ffn",
    )(block_exp, n_valid, n_used, xb, wg, bg, wl, bl, wd, bd)


def _combine_kernel(dest_ref, dnext_ref, gate_ref, h_ref, g2_ref, b2_ref, yb_ref, of_ref, ob_ref, buf, sems,
                    *, alpha):
    i = pl.program_id(0)
    tm, D = h_ref.shape
    slot = i % 2

    def gather(d_ref, s):
        def row(r, c):
            for j in range(TOP_K):
                d = d_ref[r * TOP_K + j]
                pltpu.make_async_copy(
                    yb_ref.at[pl.ds(d, 1), :], buf.at[s, j, pl.ds(r, 1), :], sems.at[s]).start(priority=j % 2)
            return c

        lax.fori_loop(0, tm, row, 0, unroll=ROW_UNROLL)

    @pl.when(i == 0)
    def _():
        gather(dest_ref, slot)

    @pl.when(i + 1 < pl.num_programs(0))
    def _():
        gather(dnext_ref, 1 - slot)

    for j in range(TOP_K):
        pltpu.make_async_copy(yb_ref.at[pl.ds(0, tm), :], buf.at[slot, j], sems.at[slot]).wait()

    gate = gate_ref[...]
    moe = None
    for j in range(TOP_K):
        term = gate[:, j:j + 1] * buf[slot, j]
        moe = term if moe is None else moe + term
    y = _layer_norm(alpha * h_ref[...] + moe, g2_ref[...], b2_ref[...])
    of_ref[...] = y
    ob_ref[...] = y.astype(BF16)


def _combine(dest_flat, gate, h, g2, b2, yb, alpha, tm):
    R, D = h.shape
    W = yb.shape[1]
    n = R // tm
    row = pl.BlockSpec((tm, D), lambda i: (i, 0))
    vec = pl.BlockSpec((1, D), lambda i: (0, 0))
    return pl.pallas_call(
        functools.partial(_combine_kernel, alpha=alpha),
        out_shape=(jax.ShapeDtypeStruct((R, D), F32), jax.ShapeDtypeStruct((R, D), BF16)),
        grid=(n,),
        in_specs=[
            pl.BlockSpec((tm * TOP_K,), lambda i: (i,), memory_space=pltpu.SMEM),
            pl.BlockSpec((tm * TOP_K,), lambda i: (jnp.minimum(i + 1, n - 1),), memory_space=pltpu.SMEM),
            pl.BlockSpec((tm, LANES), lambda i: (i, 0)),
            row, vec, vec,
            pl.BlockSpec(memory_space=pl.ANY),
        ],
        out_specs=(row, row),
        scratch_shapes=[pltpu.VMEM((2, TOP_K, tm, W), F32), pltpu.SemaphoreType.DMA((2,))],
        compiler_params=_params(("arbitrary",), 48),
        name="moe_combine",
    )(dest_flat, dest_flat, gate, h, g2.reshape(1, D), b2.reshape(1, D), yb)


def _lane_pad(v, offset):
    return jnp.zeros((1, LANES), F32).at[0, offset:offset + v.shape[0]].set(v.astype(F32))


def kernel(x, meta_tokens, ln_in_g, ln_in_b, w_in, rg_conv_w, rg_conv_b, rg_wx, rg_bx, rg_wa, rg_ba, rg_lambda, gdn_conv_w, gdn_a_log, gdn_dt_bias, gdn_norm_w, w_branch_a, w_branch_b, w_out, ln1_g, ln1_b, w_router, b_router, w_gate_up, b_gate_up, w_down, b_down, ln2_g, ln2_b):
    B, seq, D = x.shape
    depth = w_in.shape[0]
    E = w_router.shape[2]
    DE = w_down.shape[2]
    d_rnn = D // 2
    assert D == 2 * GDN_V_DIM and d_rnn == GDN_V_DIM and E <= LANES
    alpha = float((2 * depth) ** 0.25)

    T = N_META + seq
    pad = (-T) % CHUNK
    Tp = T + pad
    R = B * Tp

    tt_ln = _pick_tile(Tp, 832, CHUNK)
    tm_in = _pick_tile(R, 1040)
    tn_in = 1024
    tt_rg = _pick_tile(Tp, 416, SUBLANES)
    tt_gdn = _pick_tile(Tp, 320, CHUNK)
    tm_mg = _pick_tile(R, 256)
    tm_moe = _pick_tile(R, 256)
    bm_ffn = 2 * tm_moe
    n_blocks = -(-(R * TOP_K) // bm_ffn) + E
    m_pad = n_blocks * bm_ffn

    h, hb = _ln_in(x, meta_tokens, ln_in_g, ln_in_b, pad, tt_ln)

    wg_all, wl_all = _deinterleave(w_gate_up.reshape(depth * E, D, 2 * DE), _pick_tile(D, 1024))
    wd_all = w_down.reshape(depth * E, DE, D)
    bg_all = b_gate_up.reshape(depth * E, 1, 2 * DE)[:, :, 0::2]
    bl_all = b_gate_up.reshape(depth * E, 1, 2 * DE)[:, :, 1::2]
    bd_all = b_down.reshape(depth * E, 1, D)

    o_rx = 0
    o_q = 2 * d_rnn
    o_z = o_q + 2 * GDN_QK_DIM + GDN_V_DIM
    o_beta = o_z + GDN_V_DIM
    o_ga = o_beta + 2 * GDN_H_V
    o_gb = o_ga + D
    block_start = jnp.arange(n_blocks, dtype=I32) * bm_ffn

    for l in range(depth):
        wl = w_in[l]
        w_main = jnp.concatenate(
            [wl[:, o_ga:o_ga + D], wl[:, o_gb:o_gb + D], wl[:, o_q:o_z], wl[:, o_rx:o_q], wl[:, o_z:o_beta]],
            axis=1).astype(BF16)
        w_small = jnp.zeros((D, LANES), F32).at[:, :2 * GDN_H_V].set(wl[:, o_beta:o_ga]).astype(BF16)
        proj, small = _inproj(hb, w_main, w_small, tm_in, tn_in)

        ya = _rglru(proj, rg_conv_w[l], rg_conv_b[l], rg_wx[l].astype(BF16), rg_bx[l], rg_wa[l].astype(BF16),
                    rg_ba[l], rg_lambda[l], B, Tp, pad, tt_rg, xcol=(2 * D + 2048) // d_rnn,
                    ycol=(2 * D + 2048) // d_rnn + 1)
        yb = _gdn(proj, small, gdn_conv_w[l], _lane_pad(gdn_a_log[l], GDN_H_V), _lane_pad(gdn_dt_bias[l], GDN_H_V),
                  gdn_norm_w[l].reshape(1, GDN_DV).astype(F32), B, Tp, pad, tt_gdn, qkvcol=2,
                  zcol=(2 * D + 2048 + 2 * d_rnn) // GDN_V_DIM)

        wr = jnp.zeros((D, LANES), F32).at[:, :E].set(w_router[l])
        br = jnp.full((1, LANES), -jnp.inf, F32).at[0, :E].set(b_router[l])
        hn, idx, gate, counts = _merge(
            ya, yb, proj, h, w_branch_a[l].astype(BF16), w_branch_b[l].astype(BF16), w_out[l].astype(BF16),
            ln1_g[l], ln1_b[l], wr, br, alpha, tm_mg, gacol=0, gbcol=1)

        cnt = counts[0, :E].astype(I32)
        padded = (cnt + bm_ffn - 1) // bm_ffn * bm_ffn
        pad_ends = jnp.cumsum(padded)
        first_row = pad_ends - cnt
        block_exp = jnp.minimum(jnp.sum((pad_ends[None, :] <= block_start[:, None]).astype(I32), axis=1), E - 1)
        onehot = (block_exp[:, None] == jnp.arange(E, dtype=I32)[None, :]).astype(I32)
        n_valid = jnp.clip(block_start + bm_ffn - jnp.sum(onehot * first_row[None, :], axis=1), 0, bm_ffn)
        n_valid = jnp.where(block_start < pad_ends[E - 1], n_valid, 0).astype(I32)
        n_used = (pad_ends[E - 1:E] // bm_ffn).astype(I32)

        dest = _dest(idx, _lane_pad(first_row, 0), tm_moe)
        dest_flat = dest[:, :TOP_K].reshape(R * TOP_K)

        xb = _dispatch(n_valid, dest_flat, hn, m_pad, tm_moe, bm_ffn)
        ybk = _ffn(block_exp + l * E, n_valid, n_used, xb, wg_all, bg_all, wl_all, bl_all, wd_all, bd_all, bm_ffn)
        h, hb = _combine(dest_flat, gate, hn, ln2_g[l], ln2_b[l], ybk, alpha, tm_moe)

    return h.reshape(B, Tp, D)[:, pad + N_META:]
```
